```python
import math
import jax, jax.numpy as jnp
from jax import lax
import numpy as np

D_MODEL = 1024
BATCH = 2
SEQ = 8192
DEPTH = 2
DEC_BATCH = 128
DEC_SEQ = 8
PAST_LEN = 2048
PAGE_SIZE = 128

N_A = DEPTH // 2
N_B = DEPTH - N_A
N_META = 16
FFN_DIM = 2816
LRU_WIDTH = D_MODEL
LRU_BLOCKS = 4
LRU_BLOCK_W = LRU_WIDTH // LRU_BLOCKS
CONV_W = 4
LRU_C = 8.0
N_HEADS = 8
HEAD_DIM = D_MODEL // (2 * N_HEADS)
ATT_W = N_HEADS * 2 * HEAD_DIM
N_BUCKETS = 32
MAX_DIST = 128
Q_BLOCK = 128
NORM_EPS = 1e-6
SUBLN_EPS = 1e-5
NEG_INF = -1e30

kernel_name = 'hawk_yoco_diffattn_decoder_step'


def rms_norm(x, g, eps=NORM_EPS):
    xf = x.astype(jnp.float32)
    y = xf * lax.rsqrt(jnp.mean(xf * xf, axis=-1, keepdims=True) + eps)
    return (y * g.astype(jnp.float32)).astype(x.dtype)


def swiglu(x, w_gate, w_up, w_down):
    return (jax.nn.silu(x @ w_gate) * (x @ w_up)) @ w_down


def causal_conv(u, buf, w, b):
    T = u.shape[1]
    up = jnp.concatenate([buf.astype(u.dtype), u], axis=1)
    y = b
    for k in range(CONV_W):
        y = y + up[:, k:k + T] * w[k]
    return y, up[:, -(CONV_W - 1):]


def rg_lru(x, h0, wa, ba, wx, bx, lam):
    B, T, C = x.shape
    f32 = jnp.float32
    xb = x.reshape(B, T, LRU_BLOCKS, LRU_BLOCK_W)
    r = jax.nn.sigmoid((jnp.einsum('btnd,nde->btne', xb, wa).reshape(B, T, C) + ba).astype(f32))
    i = jax.nn.sigmoid((jnp.einsum('btnd,nde->btne', xb, wx).reshape(B, T, C) + bx).astype(f32))
    log_a = LRU_C * r * jax.nn.log_sigmoid(lam.astype(f32))
    a = jnp.exp(log_a)
    b = jnp.sqrt(-jnp.expm1(2.0 * log_a)) * (i * x.astype(f32))

    def step(h, ab):
        a_t, b_t = ab
        h = a_t * h + b_t
        return h, h

    hT, hs = lax.scan(step, h0.astype(f32), (jnp.swapaxes(a, 0, 1), jnp.swapaxes(b, 0, 1)))
    return jnp.swapaxes(hs, 0, 1).astype(x.dtype), hT.astype(x.dtype)


def recurrent_block(x, conv_buf, h0, w_y, w_x, conv_w, conv_b, wa, ba, wx, bx, lam, w_out):
    y_branch = jax.nn.gelu(x @ w_y)
    u = x @ w_x
    u_conv, new_buf = causal_conv(u, conv_buf, conv_w, conv_b)
    h, hT = rg_lru(u_conv, h0, wa, ba, wx, bx, lam)
    return (h * y_branch) @ w_out, new_buf, hT


def t5_bucket(dist):
    n = jnp.maximum(dist, 0)
    max_exact = N_BUCKETS // 2
    nf = jnp.maximum(n, 1).astype(jnp.float32)
    large = max_exact + (jnp.log(nf / max_exact) / math.log(MAX_DIST / max_exact)
                         * (N_BUCKETS - max_exact)).astype(jnp.int32)
    large = jnp.minimum(large, N_BUCKETS - 1)
    return jnp.where(n < max_exact, n, large)


def diff_logits(q, k, q_pos, k_pos, rel_bias):
    s = jnp.einsum('bqhcd,bkhcd->bchqk', q, k).astype(jnp.float32) * (HEAD_DIM ** -0.5)
    dist = q_pos[:, None] - k_pos[None, :]
    bias = jnp.transpose(rel_bias[t5_bucket(dist)].astype(jnp.float32), (2, 0, 1))
    s = s + bias
    return jnp.where(dist >= 0, s, NEG_INF)


def diff_attend(q, segs, q_pos, rel_bias, lam):
    s = jnp.concatenate([diff_logits(q, k, q_pos, kp, rel_bias) for k, _, kp in segs], axis=-1)
    p = jax.nn.softmax(s, axis=-1)
    w = (p[:, 0] - lam * p[:, 1]).astype(segs[0][1].dtype)
    out = None
    off = 0
    for k, v, _ in segs:
        n = k.shape[1]
        o = jnp.einsum('bhqk,bkhe->bqhe', w[..., off:off + n], v)
        out = o if out is None else out + o
        off += n
    return out


def blocked_prompt_attend(q, k, v, rel_bias, lam):
    B, T = q.shape[:2]
    n_blk = -(-T // Q_BLOCK)
    T_pad = n_blk * Q_BLOCK
    qp = jnp.pad(q, ((0, 0), (0, T_pad - T), (0, 0), (0, 0), (0, 0)))
    qb = jnp.moveaxis(qp.reshape(B, n_blk, Q_BLOCK, N_HEADS, 2, HEAD_DIM), 1, 0)
    k_pos = jnp.arange(T, dtype=jnp.int32)
    starts = jnp.arange(n_blk, dtype=jnp.int32) * Q_BLOCK

    def one(args):
        q_blk, s0 = args
        q_pos = s0 + jnp.arange(Q_BLOCK, dtype=jnp.int32)
        return diff_attend(q_blk, ((k, v, k_pos),), q_pos, rel_bias, lam)

    o = lax.map(one, (qb, starts))
    o = jnp.moveaxis(o, 0, 1).reshape(B, T_pad, N_HEADS, 2 * HEAD_DIM)
    return o[:, :T]


def diff_attn_mixer(h, k_sh, v_sh, past, lam_init, w_q, w_o, lq1, lk1, lq2, lk2, subln_g, rel_bias):
    B, T, _ = h.shape
    f32 = jnp.float32
    q = (h @ w_q).reshape(B, T, N_HEADS, 2, HEAD_DIM)
    lam = (jnp.exp(jnp.sum(lq1.astype(f32) * lk1.astype(f32)))
           - jnp.exp(jnp.sum(lq2.astype(f32) * lk2.astype(f32))) + lam_init)
    if past is None:
        o = blocked_prompt_attend(q, k_sh, v_sh, rel_bias, lam)
    else:
        k_past, v_past = past
        P = k_past.shape[1]
        past_pos = jnp.arange(P, dtype=jnp.int32)
        q_pos = P + jnp.arange(T, dtype=jnp.int32)
        o = diff_attend(q, ((k_past, v_past, past_pos), (k_sh, v_sh, q_pos)), q_pos, rel_bias, lam)
    o = rms_norm(o, subln_g, SUBLN_EPS) * (1.0 - lam_init)
    return o.reshape(B, T, ATT_W) @ w_o


def trunk(x, conv0, h0, past, p):
    B, T, _ = x.shape
    conv_out, h_out = [], []
    k_sh, v_sh = None, None
    for l in range(DEPTH):
        g = p['norm_g'][l]
        x = x + 0.5 * rms_norm(swiglu(rms_norm(x, g[0]), p['ffn_w_gate'][l, 0],
                                      p['ffn_w_up'][l, 0], p['ffn_w_down'][l, 0]), g[1])
        h = rms_norm(x, g[2])
        if l < N_A:
            m, cb, hT = recurrent_block(h, conv0[l], h0[l], p['lru_w_y'][l], p['lru_w_x'][l],
                                        p['lru_conv_w'][l], p['lru_conv_b'][l], p['lru_wa'][l],
                                        p['lru_ba'][l], p['lru_wx'][l], p['lru_bx'][l],
                                        p['lru_lambda'][l], p['lru_w_out'][l])
            conv_out.append(cb)
            h_out.append(hT)
        else:
            j = l - N_A
            lam_init = 0.8 - 0.6 * math.exp(-0.3 * l)
            m = diff_attn_mixer(h, k_sh, v_sh, past, lam_init, p['attn_w_q'][j], p['attn_w_o'][j],
                                p['attn_lambda_q1'][j], p['attn_lambda_k1'][j],
                                p['attn_lambda_q2'][j], p['attn_lambda_k2'][j],
                                p['attn_subln_g'][j], p['rel_bias'])
        x = x + rms_norm(m, g[3])
        x = x + 0.5 * rms_norm(swiglu(rms_norm(x, g[4]), p['ffn_w_gate'][l, 1],
                                      p['ffn_w_up'][l, 1], p['ffn_w_down'][l, 1]), g[5])
        if l == N_A - 1:
            hk = rms_norm(x, p['kv_norm_g'])
            k_sh = (hk @ p['attn_w_k']).reshape(B, T, N_HEADS, 2, HEAD_DIM)
            v_sh = (hk @ p['attn_w_v']).reshape(B, T, N_HEADS, 2 * HEAD_DIM)
    return x, k_sh, v_sh, jnp.stack(conv_out), jnp.stack(h_out)


def setup_inputs(seed: int = 0) -> dict:
    key = jax.random.key(seed)
    ks = jax.random.split(key, 40)
    f32 = jnp.float32

    def nrm(k, shape, scale):
        return jax.random.normal(k, shape, f32) * scale

    n_pages = PAST_LEN // PAGE_SIZE
    n_used = DEC_BATCH * n_pages
    n_pool = n_used + max(n_used // 4, 1)
    perm = jax.random.permutation(ks[0], n_pool)
    page_table = perm[:n_used].reshape(DEC_BATCH, n_pages).astype(jnp.int32)

    u = jax.random.uniform(ks[1], (N_A, LRU_WIDTH), f32, 0.9, 0.999)
    a_base = u ** (1.0 / LRU_C)
    lru_lambda = jnp.log(a_base) - jnp.log1p(-a_base)

    return {
        'x_prompt': nrm(ks[2], (BATCH, SEQ, D_MODEL), 1.0),
        'x_sample': nrm(ks[3], (DEC_BATCH, DEC_SEQ, D_MODEL), 1.0),
        'cache_k': nrm(ks[4], (n_pool, PAGE_SIZE, N_HEADS, 2, HEAD_DIM), 1.0),
        'cache_v': nrm(ks[5], (n_pool, PAGE_SIZE, N_HEADS, 2 * HEAD_DIM), 1.0),
        'page_table': page_table,
        'state_conv': nrm(ks[6], (N_A, DEC_BATCH, CONV_W - 1, LRU_WIDTH), 1.0),
        'state_h': nrm(ks[7], (N_A, DEC_BATCH, LRU_WIDTH), 0.5),
        'meta_tokens': nrm(ks[8], (N_META, D_MODEL), 1.0),
        'norm_g': 1.0 + nrm(ks[9], (DEPTH, 6, D_MODEL), 0.05),
        'kv_norm_g': 1.0 + nrm(ks[10], (D_MODEL,), 0.05),
        'ffn_w_gate': nrm(ks[11], (DEPTH, 2, D_MODEL, FFN_DIM), D_MODEL ** -0.5),
        'ffn_w_up': nrm(ks[12], (DEPTH, 2, D_MODEL, FFN_DIM), D_MODEL ** -0.5),
        'ffn_w_down': nrm(ks[13], (DEPTH, 2, FFN_DIM, D_MODEL), FFN_DIM ** -0.5),
        'lru_w_y': nrm(ks[14], (N_A, D_MODEL, LRU_WIDTH), D_MODEL ** -0.5),
        'lru_w_x': nrm(ks[15], (N_A, D_MODEL, LRU_WIDTH), D_MODEL ** -0.5),
        'lru_conv_w': nrm(ks[16], (N_A, CONV_W, LRU_WIDTH), CONV_W ** -0.5),
        'lru_conv_b': nrm(ks[17], (N_A, LRU_WIDTH), 0.01),
        'lru_wa': nrm(ks[18], (N_A, LRU_BLOCKS, LRU_BLOCK_W, LRU_BLOCK_W), LRU_BLOCK_W ** -0.5),
        'lru_ba': nrm(ks[19], (N_A, LRU_WIDTH), 0.01),
        'lru_wx': nrm(ks[20], (N_A, LRU_BLOCKS, LRU_BLOCK_W, LRU_BLOCK_W), LRU_BLOCK_W ** -0.5),
        'lru_bx': nrm(ks[21], (N_A, LRU_WIDTH), 0.01),
        'lru_lambda': lru_lambda,
        'lru_w_out': nrm(ks[22], (N_A, LRU_WIDTH, D_MODEL), LRU_WIDTH ** -0.5),
        'attn_w_q': nrm(ks[23], (N_B, D_MODEL, ATT_W), D_MODEL ** -0.5),
        'attn_w_k': nrm(ks[24], (D_MODEL, ATT_W), D_MODEL ** -0.5),
        'attn_w_v': nrm(ks[25], (D_MODEL, ATT_W), D_MODEL ** -0.5),
        'attn_w_o': nrm(ks[26], (N_B, ATT_W, D_MODEL), ATT_W ** -0.5),
        'attn_lambda_q1': nrm(ks[27], (N_B, HEAD_DIM), 0.1),
        'attn_lambda_k1': nrm(ks[28], (N_B, HEAD_DIM), 0.1),
        'attn_lambda_q2': nrm(ks[29], (N_B, HEAD_DIM), 0.1),
        'attn_lambda_k2': nrm(ks[30], (N_B, HEAD_DIM), 0.1),
        'attn_subln_g': 1.0 + nrm(ks[31], (N_B, 2 * HEAD_DIM), 0.05),
        'rel_bias': nrm(ks[32], (N_BUCKETS, N_HEADS), 0.5),
    }


def reference(x_prompt, x_sample, cache_k, cache_v, page_table, state_conv, state_h,
              meta_tokens, norm_g, kv_norm_g, ffn_w_gate, ffn_w_up, ffn_w_down,
              lru_w_y, lru_w_x, lru_conv_w, lru_conv_b, lru_wa, lru_ba, lru_wx, lru_bx,
              lru_lambda, lru_w_out, attn_w_q, attn_w_k, attn_w_v, attn_w_o,
              attn_lambda_q1, attn_lambda_k1, attn_lambda_q2, attn_lambda_k2,
              attn_subln_g, rel_bias):
    p = {
        'norm_g': norm_g, 'kv_norm_g': kv_norm_g,
        'ffn_w_gate': ffn_w_gate, 'ffn_w_up': ffn_w_up, 'ffn_w_down': ffn_w_down,
        'lru_w_y': lru_w_y, 'lru_w_x': lru_w_x, 'lru_conv_w': lru_conv_w, 'lru_conv_b': lru_conv_b,
        'lru_wa': lru_wa, 'lru_ba': lru_ba, 'lru_wx': lru_wx, 'lru_bx': lru_bx,
        'lru_lambda': lru_lambda, 'lru_w_out': lru_w_out,
        'attn_w_q': attn_w_q, 'attn_w_k': attn_w_k, 'attn_w_v': attn_w_v, 'attn_w_o': attn_w_o,
        'attn_lambda_q1': attn_lambda_q1, 'attn_lambda_k1': attn_lambda_k1,
        'attn_lambda_q2': attn_lambda_q2, 'attn_lambda_k2': attn_lambda_k2,
        'attn_subln_g': attn_subln_g, 'rel_bias': rel_bias,
    }
    B = x_prompt.shape[0]
    meta = jnp.broadcast_to(meta_tokens.astype(x_prompt.dtype)[None], (B, N_META, D_MODEL))
    xp = jnp.concatenate([meta, x_prompt], axis=1)
    conv0 = jnp.zeros((N_A, B, CONV_W - 1, LRU_WIDTH), x_prompt.dtype)
    h0 = jnp.zeros((N_A, B, LRU_WIDTH), x_prompt.dtype)
    yp, k_p, v_p, conv_p, h_p = trunk(xp, conv0, h0, None, p)
    y_prompt = yp[:, N_META:]

    DB = x_sample.shape[0]
    k_past = cache_k[page_table].reshape(DB, PAST_LEN, N_HEADS, 2, HEAD_DIM)
    v_past = cache_v[page_table].reshape(DB, PAST_LEN, N_HEADS, 2 * HEAD_DIM)
    y_sample, k_s, v_s, conv_s, h_s = trunk(x_sample, state_conv, state_h, (k_past, v_past), p)
    return (y_prompt, y_sample, k_p, v_p, conv_p, h_p, k_s, v_s, conv_s, h_s)
```

```python
import functools
import math

import numpy as np
import jax
import jax.numpy as jnp
from jax import lax
from jax.experimental import pallas as pl
from jax.experimental.pallas import tpu as pltpu

f32 = jnp.float32
bf16 = jnp.bfloat16

N_META = 16
N_HEADS = 8
HEAD_DIM = 64
CONV_W = 4
LRU_BLOCKS = 4
LRU_C = 8.0
N_BUCKETS = 32
MAX_DIST = 128
NORM_EPS = 1e-6
SUBLN_EPS = 1e-5
NEG_INF = -1e30
PAGE = 128

V7X_VMEM_BYTES = 64 * 1024 * 1024
VMEM_LIMIT = V7X_VMEM_BYTES - 8 * 1024 * 1024
SUBLANES = 8
LANES = 128

ROW_TILE = 512
SEQ_TILE = 256


def _cparams(n_axes):
    return pltpu.CompilerParams(dimension_semantics=("arbitrary",) * n_axes,
                                vmem_limit_bytes=VMEM_LIMIT)


def _const_spec(shape):
    nd = len(shape)
    return pl.BlockSpec(shape, lambda *_: (0,) * nd, pipeline_mode=pl.Buffered(1))


def _rms(x, g, eps):
    ms = jnp.mean(x * x, axis=-1, keepdims=True)
    return x * lax.rsqrt(ms + eps) * g


def _dot(a, b):
    return jnp.dot(a, b, preferred_element_type=f32)


def _dot_nt(a, b):
    return lax.dot_general(a, b, (((1,), (1,)), ((), ())), preferred_element_type=f32)


def _ffn_kernel(x_ref, g_ref, wg_ref, wu_ref, wd_ref, o_ref):
    x = x_ref[...]
    xn = _rms(x, g_ref[0:1, :], NORM_EPS).astype(bf16)
    hg = _dot(xn, wg_ref[...])
    hu = _dot(xn, wu_ref[...])
    act = (jax.nn.silu(hg) * hu).astype(bf16)
    y = _dot(act, wd_ref[...])
    o_ref[...] = x + 0.5 * _rms(y, g_ref[1:2, :], NORM_EPS)


def _ffn(x, g2, wg, wu, wd):
    rows, d = x.shape
    ffn = wg.shape[1]
    row = pl.BlockSpec((ROW_TILE, d), lambda i: (i, 0))
    return pl.pallas_call(
        _ffn_kernel,
        grid=(rows // ROW_TILE,),
        in_specs=[row, _const_spec((2, d)), _const_spec((d, ffn)), _const_spec((d, ffn)),
                  _const_spec((ffn, d))],
        out_specs=row,
        out_shape=jax.ShapeDtypeStruct((rows, d), f32),
        compiler_params=_cparams(1),
        name="ffn",
    )(x, g2, wg, wu, wd)


def _lru_in_kernel(x_ref, g_ref, wy_ref, wx_ref, yb_ref, u_ref):
    h = _rms(x_ref[...], g_ref[...], NORM_EPS).astype(bf16)
    yb_ref[...] = jax.nn.gelu(_dot(h, wy_ref[...]))
    u_ref[...] = _dot(h, wx_ref[...])


def _lru_in(x, g, wy, wx):
    rows, d = x.shape
    row = pl.BlockSpec((ROW_TILE, d), lambda i: (i, 0))
    return pl.pallas_call(
        _lru_in_kernel,
        grid=(rows // ROW_TILE,),
        in_specs=[row, _const_spec((1, d)), _const_spec((d, d)), _const_spec((d, d))],
        out_specs=[row, row],
        out_shape=[jax.ShapeDtypeStruct((rows, d), f32)] * 2,
        compiler_params=_cparams(1),
        name="lru_in",
    )(x, g, wy, wx)


def _proj_res_kernel(a_ref, x_ref, w_ref, g_ref, o_ref):
    m = _dot(a_ref[...].astype(bf16), w_ref[...])
    o_ref[...] = x_ref[...] + _rms(m, g_ref[...], NORM_EPS)


def _proj_res(a, x, w, g):
    rows, d = x.shape
    row = pl.BlockSpec((ROW_TILE, d), lambda i: (i, 0))
    return pl.pallas_call(
        _proj_res_kernel,
        grid=(rows // ROW_TILE,),
        in_specs=[row, row, _const_spec((d, d)), _const_spec((1, d))],
        out_specs=row,
        out_shape=jax.ShapeDtypeStruct((rows, d), f32),
        compiler_params=_cparams(1),
        name="proj_res",
    )(a, x, w, g)


def _kv_kernel(x_ref, g_ref, wk_ref, wv_ref, k_ref, v_ref, kb_ref, vb_ref):
    hk = _rms(x_ref[...], g_ref[...], NORM_EPS).astype(bf16)
    k = _dot(hk, wk_ref[...])
    v = _dot(hk, wv_ref[...])
    k_ref[...] = k
    v_ref[...] = v
    kb_ref[...] = k.astype(bf16)
    vb_ref[...] = v.astype(bf16)


def _kv(x, g, wk, wv):
    rows, d = x.shape
    row = pl.BlockSpec((ROW_TILE, d), lambda i: (i, 0))
    return pl.pallas_call(
        _kv_kernel,
        grid=(rows // ROW_TILE,),
        in_specs=[row, _const_spec((1, d)), _const_spec((d, d)), _const_spec((d, d))],
        out_specs=[row] * 4,
        out_shape=[jax.ShapeDtypeStruct((rows, d), f32)] * 2
        + [jax.ShapeDtypeStruct((rows, d), bf16)] * 2,
        compiler_params=_cparams(1),
        name="kv_proj",
    )(x, g, wk, wv)


def _q_kernel(x_ref, g_ref, wq_ref, q_ref):
    h = _rms(x_ref[...], g_ref[...], NORM_EPS).astype(bf16)
    q_ref[...] = _dot(h, wq_ref[...]) * (HEAD_DIM ** -0.5)


def _q_proj(x, g, wq):
    rows, d = x.shape
    row = pl.BlockSpec((ROW_TILE, d), lambda i: (i, 0))
    return pl.pallas_call(
        _q_kernel,
        grid=(rows // ROW_TILE,),
        in_specs=[row, _const_spec((1, d)), _const_spec((d, d))],
        out_specs=row,
        out_shape=jax.ShapeDtypeStruct((rows, d), f32),
        compiler_params=_cparams(1),
        name="q_proj",
    )(x, g, wq)


def _log_sigmoid(x):
    return -(jnp.maximum(-x, 0.0) + jnp.log1p(jnp.exp(-jnp.abs(x))))


def _scan_rows(a, b, t, shifts):
    for s in shifts:
        keep = t >= s
        a_s = jnp.where(keep, pltpu.roll(a, s, 0), 1.0)
        b_s = jnp.where(keep, pltpu.roll(b, s, 0), 0.0)
        b = a * b_s + b
        a = a * a_s
    return a, b


def _scan_kernel(n_seq_tiles, n_prompt_tiles, last_tile, last_row,
                 u_ref, yb_ref, cb_ref, h0_ref, cw_ref, vec_ref, wa_ref, wx_ref,
                 hy_ref, hp_ref, hs_ref, ucar_ref, hcar_ref):
    s = pl.program_id(0)
    rows, c = u_ref.shape
    bw = c // LRU_BLOCKS
    u = u_ref[...]
    row = lax.broadcasted_iota(jnp.int32, (rows, 1), 0)
    conv_b, ba, bx, lam = (vec_ref[i:i + 1, :] for i in range(4))

    def gates_and_ab(uc):
        ucb = uc.astype(bf16)
        rp = jnp.concatenate([_dot(ucb[:, n * bw:(n + 1) * bw], wa_ref[n]) for n in range(LRU_BLOCKS)], axis=1)
        ip = jnp.concatenate([_dot(ucb[:, n * bw:(n + 1) * bw], wx_ref[n]) for n in range(LRU_BLOCKS)], axis=1)
        r = jax.nn.sigmoid(rp + ba)
        i = jax.nn.sigmoid(ip + bx)
        log_a = LRU_C * r * _log_sigmoid(lam)
        a = jnp.exp(log_a)
        b = jnp.sqrt(-jnp.tanh(log_a) * (a * a + 1.0)) * (i * uc)
        return a, b

    @pl.when(s < n_prompt_tiles)
    def _prompt():
        i = jnp.where(s >= n_seq_tiles, s - n_seq_tiles, s)

        @pl.when(i == 0)
        def _():
            ucar_ref[...] = jnp.zeros_like(ucar_ref)
            hcar_ref[...] = jnp.zeros_like(hcar_ref)

        prev = ucar_ref[...]
        t8 = row[0:SUBLANES]
        uc = conv_b + cw_ref[CONV_W - 1:CONV_W, :] * u
        for j in range(1, CONV_W):
            uj = pltpu.roll(u, j, 0)
            head = jnp.where(t8 < j, pltpu.roll(prev, j, 0), uj[0:SUBLANES])
            uj = jnp.concatenate([head, uj[SUBLANES:]], axis=0)
            uc = uc + cw_ref[CONV_W - 1 - j:CONV_W - j, :] * uj
        a, b = gates_and_ab(uc)
        shifts = [1 << k for k in range(int(math.log2(rows)))]
        a, b = _scan_rows(a, b, row, shifts)
        h = a * hcar_ref[...] + b
        hy_ref[...] = (h * yb_ref[...]).astype(hy_ref.dtype)
        ucar_ref[...] = u[rows - SUBLANES:rows]
        hcar_ref[...] = h[rows - 1:rows]

        @pl.when(i == last_tile)
        def _():
            hp_ref[0] = h[last_row:last_row + 1]

    @pl.when(s >= n_prompt_tiles)
    def _sample():
        t = row % SUBLANES
        cb = cb_ref[...]
        uc = conv_b + cw_ref[CONV_W - 1:CONV_W, :] * u
        for j in range(1, CONV_W):
            uj = jnp.where(t >= j, pltpu.roll(u, j, 0), pltpu.roll(cb, rows - SUBLANES + j, 0))
            uc = uc + cw_ref[CONV_W - 1 - j:CONV_W - j, :] * uj
        a, b = gates_and_ab(uc)
        a, b = _scan_rows(a, b, t, [1, 2, 4])
        h = a * h0_ref[...] + b
        hy_ref[...] = (h * yb_ref[...]).astype(hy_ref.dtype)
        hs_ref[...] = h


def _scan(u, yb, cb_rows, h0_rows, conv_w, vecs, wa, wx, n_seq, t_pad, t_real):
    rows, c = u.shape
    n_seq_tiles = t_pad // SEQ_TILE
    n_prompt_tiles = n_seq * n_seq_tiles
    n_tiles = rows // SEQ_TILE
    bw = c // LRU_BLOCKS
    tile = pl.BlockSpec((SEQ_TILE, c), lambda s: (s, 0))
    samp = pl.BlockSpec((SEQ_TILE, c), lambda s: (jnp.maximum(s - n_prompt_tiles, 0), 0))
    kern = functools.partial(_scan_kernel, n_seq_tiles, n_prompt_tiles,
                             (t_real - 1) // SEQ_TILE, (t_real - 1) % SEQ_TILE)
    return pl.pallas_call(
        kern,
        grid=(n_tiles,),
        in_specs=[tile, tile, samp, samp, _const_spec((CONV_W, c)), _const_spec((4, c)),
                  _const_spec((LRU_BLOCKS, bw, bw)), _const_spec((LRU_BLOCKS, bw, bw))],
        out_specs=[tile,
                   pl.BlockSpec((1, 1, c), lambda s: (jnp.minimum(s // n_seq_tiles, n_seq - 1), 0, 0)),
                   samp],
        out_shape=[jax.ShapeDtypeStruct((rows, c), bf16),
                   jax.ShapeDtypeStruct((n_seq, 1, c), f32),
                   jax.ShapeDtypeStruct((rows - n_prompt_tiles * SEQ_TILE, c), f32)],
        scratch_shapes=[pltpu.VMEM((SUBLANES, c), f32), pltpu.VMEM((1, c), f32)],
        compiler_params=_cparams(1),
        name="lru_scan",
    )(u, yb, cb_rows, h0_rows, conv_w, vecs, wa, wx)


def _bucket_last_dist():
    n = np.arange(0, 4 * MAX_DIST, dtype=np.int32)
    max_exact = N_BUCKETS // 2
    nf = np.maximum(n, 1).astype(np.float32)
    large = max_exact + (np.log(nf / np.float32(max_exact)) / np.float32(math.log(MAX_DIST / max_exact))
                         * np.float32(N_BUCKETS - max_exact)).astype(np.int32)
    bucket = np.where(n < max_exact, n, np.minimum(large, N_BUCKETS - 1))
    assert np.all(np.diff(bucket) >= 0)
    last = [int(n[bucket == b].max()) if np.any(bucket == b) else None for b in range(N_BUCKETS)]
    assert last[N_BUCKETS - 1] == n[-1]
    return last


_BUCKET_LAST = _bucket_last_dist()
FAR_DIST = max(d for d in _BUCKET_LAST[:-1] if d is not None) + 1


def _bias_of_dist(dist, rb_ref, h):
    val = jnp.full(dist.shape, rb_ref[N_BUCKETS - 1, h], f32)
    for b in range(N_BUCKETS - 2, -1, -1):
        if _BUCKET_LAST[b] is not None:
            val = jnp.where(dist <= _BUCKET_LAST[b], rb_ref[b, h], val)
    return val


def _bias_kernel(past_len, rb_ref, g_ref, bs_ref):
    h = pl.program_id(0)
    _, tq, w = g_ref.shape
    qi = lax.broadcasted_iota(jnp.int32, (tq, w), 0)
    ci = lax.broadcasted_iota(jnp.int32, (tq, w), 1)
    g_ref[0] = _bias_of_dist(qi - ci + (w - tq), rb_ref, h)
    _, _, nq, wk = bs_ref.shape
    q = lax.broadcasted_iota(jnp.int32, (nq, wk), 0)
    col = lax.broadcasted_iota(jnp.int32, (nq, wk), 1)
    bias = _bias_of_dist(past_len + q - col, rb_ref, h)
    bs_ref[0, 0] = bias
    bs_ref[1, 0] = bias


def _bias_tables(rel_bias, past_len, n_new):
    g_shape = (N_HEADS, SEQ_TILE, 2 * SEQ_TILE)
    bs_shape = (2, N_HEADS, n_new, past_len + PAGE)
    return pl.pallas_call(
        functools.partial(_bias_kernel, past_len),
        grid=(N_HEADS,),
        in_specs=[pl.BlockSpec(memory_space=pltpu.SMEM)],
        out_specs=[pl.BlockSpec((1,) + g_shape[1:], lambda h: (h, 0, 0)),
                   pl.BlockSpec((2, 1) + bs_shape[2:], lambda h: (0, h, 0, 0))],
        out_shape=[jax.ShapeDtypeStruct(g_shape, f32), jax.ShapeDtypeStruct(bs_shape, f32)],
        compiler_params=_cparams(1),
        name="rel_bias_tables",
    )(rel_bias)


def _lambda(lamv_ref, lam_init):
    l1 = jnp.sum(lamv_ref[0:1, :] * lamv_ref[1:2, :], axis=-1, keepdims=True)
    l2 = jnp.sum(lamv_ref[2:3, :] * lamv_ref[3:4, :], axis=-1, keepdims=True)
    return jnp.exp(l1) - jnp.exp(l2) + lam_init


def _attn_prompt_kernel(lam_init, rb_ref, q_ref, k_ref, v_ref, g_ref, lamv_ref, sg_ref, o_ref):
    h = pl.program_id(1)
    t_pad, hw = q_ref.shape
    tq = SEQ_TILE
    n_q = t_pad // tq
    far_bias = rb_ref[N_BUCKETS - 1, h]
    lam = _lambda(lamv_ref, lam_init)
    lane = lax.broadcasted_iota(jnp.int32, (tq, hw), 1)
    qi = lax.broadcasted_iota(jnp.int32, (tq, tq), 0)
    ki = lax.broadcasted_iota(jnp.int32, (tq, tq), 1)
    causal = jnp.concatenate([qi >= ki, qi >= ki], axis=0)

    def q_tile(i, _):
        q = q_ref[pl.ds(pl.multiple_of(i * tq, tq), tq), :]
        qq = jnp.concatenate([jnp.where(lane < HEAD_DIM, q, 0.0), jnp.where(lane >= HEAD_DIM, q, 0.0)],
                             axis=0).astype(bf16)

        def update(carry, j, s):
            m, l, acc = carry
            v = v_ref[pl.ds(pl.multiple_of(j * tq, tq), tq), :]
            m_new = jnp.maximum(m, jnp.max(s, axis=-1, keepdims=True))
            alpha = jnp.exp(m - m_new)
            p = jnp.exp(s - m_new)
            l = alpha * l + jnp.sum(p, axis=-1, keepdims=True)
            acc = alpha * acc + _dot(p.astype(bf16), v)
            return m_new, l, acc

        def logits(j):
            k = k_ref[pl.ds(pl.multiple_of(j * tq, tq), tq), :]
            return _dot_nt(qq, k)

        def far_block(j, carry):
            return update(carry, j, logits(j) + far_bias)

        def near_block(j, carry):
            g = g_ref[0, :, 0:tq]
            return update(carry, j, logits(j) + jnp.concatenate([g, g], axis=0))

        carry = (jnp.full((2 * tq, 1), NEG_INF, f32), jnp.zeros((2 * tq, 1), f32),
                 jnp.zeros((2 * tq, hw), f32))
        carry = lax.fori_loop(0, jnp.maximum(i - 1, 0), far_block, carry)
        carry = lax.fori_loop(jnp.maximum(i - 1, 0), i, near_block, carry)
        g = g_ref[0, :, tq:2 * tq]
        s = jnp.where(causal, logits(i) + jnp.concatenate([g, g], axis=0), NEG_INF)
        m, l, acc = update(carry, i, s)
        on = acc / l
        o = on[0:tq] - lam * on[tq:2 * tq]
        o = _rms(o, sg_ref[...], SUBLN_EPS) * (1.0 - lam_init)
        o_ref[pl.ds(pl.multiple_of(i * tq, tq), tq), :] = o
        return 0

    lax.fori_loop(0, n_q, q_tile, 0)


def _attn_prompt(q, kb, vb, g_tiles, rel_bias, lamv, subln_g, n_seq, t_pad, lam_init):
    rows, d = q.shape
    hw = d // N_HEADS
    assert SEQ_TILE >= FAR_DIST
    blk = lambda b, h: (b, h)
    return pl.pallas_call(
        functools.partial(_attn_prompt_kernel, lam_init),
        grid=(n_seq, N_HEADS),
        in_specs=[pl.BlockSpec(memory_space=pltpu.SMEM),
                  pl.BlockSpec((t_pad, hw), blk), pl.BlockSpec((t_pad, hw), blk),
                  pl.BlockSpec((t_pad, hw), blk),
                  pl.BlockSpec((1,) + g_tiles.shape[1:], lambda b, h: (h, 0, 0)),
                  _const_spec(lamv.shape), _const_spec(subln_g.shape)],
        out_specs=pl.BlockSpec((t_pad, hw), blk),
        out_shape=jax.ShapeDtypeStruct((rows, d), f32),
        compiler_params=_cparams(2),
        name="attn_prompt",
    )(rel_bias, q, kb, vb, g_tiles, lamv, subln_g)


def _attn_sample_kernel(lam_init, n_pages, pt_ref, q_ref, kn_ref, vn_ref, bias_ref, lamv_ref, sg_ref,
                        *refs):
    k_refs = refs[:n_pages]
    v_refs = refs[n_pages:2 * n_pages]
    o_ref = refs[2 * n_pages + 1]
    n_new, d = q_ref.shape
    n_grp = 2 * N_HEADS
    n_rows = n_grp * n_new
    hw = d // N_HEADS
    lam = _lambda(lamv_ref, lam_init)

    qt = jnp.concatenate([q_ref[...]] * n_grp, axis=0)
    r = lax.broadcasted_iota(jnp.int32, (n_rows, d), 0)
    col = lax.broadcasted_iota(jnp.int32, (n_rows, d), 1)
    row_grp = ((r // n_new) % N_HEADS) * 2 + r // (N_HEADS * n_new)
    wq = jnp.where(col // HEAD_DIM == row_grp, qt, 0.0).astype(bf16)

    pad = jnp.zeros((PAGE - n_new, d), f32)
    k_new = jnp.concatenate([kn_ref[...], pad], axis=0).astype(bf16)
    v_new = jnp.concatenate([vn_ref[...], pad], axis=0).astype(bf16)

    s = jnp.concatenate([_dot(wq, k_refs[j][0].astype(bf16)) for j in range(n_pages)]
                        + [_dot_nt(wq, k_new)], axis=1)
    s = s + bias_ref[...]
    past = n_pages * PAGE
    rr = lax.broadcasted_iota(jnp.int32, s.shape, 0)
    cc = lax.broadcasted_iota(jnp.int32, s.shape, 1)
    s = jnp.where(cc - past <= rr % n_new, s, NEG_INF)
    m = jnp.max(s, axis=-1, keepdims=True)
    p = jnp.exp(s - m)
    p = p / jnp.sum(p, axis=-1, keepdims=True)
    half = N_HEADS * n_new
    w = (p[0:half] - lam * p[half:2 * half]).astype(bf16)
    sg = sg_ref[...]
    heads = []
    for hh in range(N_HEADS):
        vh = jnp.concatenate([v_refs[j][0, pl.ds(hh, PAGE, stride=N_HEADS), :].astype(bf16)
                              for j in range(n_pages)] + [v_new[:, hh * hw:(hh + 1) * hw]], axis=0)
        out = _dot(w[hh * n_new:(hh + 1) * n_new, :], vh)
        heads.append(_rms(out, sg, SUBLN_EPS))
    o_ref[...] = jnp.concatenate(heads, axis=1) * (1.0 - lam_init)


def _attn_sample(q, k_new, v_new, cache_k, cache_v, page_table, bias_s, lamv, subln_g, o_all,
                 row0, n_new, lam_init):
    n_batch, n_pages = page_table.shape
    d = q.shape[1]
    blk0 = row0 // n_new
    tok = pl.BlockSpec((n_new, d), lambda b, pt: (blk0 + b, 0))

    def page_spec(j):
        return pl.BlockSpec((1,) + cache_k.shape[1:], lambda b, pt: (pt[b, j], 0, 0))

    grid_spec = pltpu.PrefetchScalarGridSpec(
        num_scalar_prefetch=1,
        grid=(n_batch,),
        in_specs=[tok, tok, tok,
                  pl.BlockSpec(bias_s.shape, lambda b, pt: (0, 0), pipeline_mode=pl.Buffered(1)),
                  pl.BlockSpec(lamv.shape, lambda b, pt: (0, 0), pipeline_mode=pl.Buffered(1)),
                  pl.BlockSpec(subln_g.shape, lambda b, pt: (0, 0), pipeline_mode=pl.Buffered(1))]
        + [page_spec(j) for j in range(n_pages)] * 2
        + [pl.BlockSpec(memory_space=pl.ANY)],
        out_specs=tok,
    )
    n_in = 1 + 6 + 2 * n_pages
    return pl.pallas_call(
        functools.partial(_attn_sample_kernel, lam_init, n_pages),
        grid_spec=grid_spec,
        out_shape=jax.ShapeDtypeStruct(o_all.shape, o_all.dtype),
        input_output_aliases={n_in: 0},
        compiler_params=_cparams(1),
        name="attn_sample",
    )(page_table, q, k_new, v_new, bias_s, lamv, subln_g,
      *([cache_k] * n_pages), *([cache_v] * n_pages), o_all)


def kernel(x_prompt, x_sample, cache_k, cache_v, page_table, state_conv, state_h, meta_tokens, norm_g, kv_norm_g, ffn_w_gate, ffn_w_up, ffn_w_down, lru_w_y, lru_w_x, lru_conv_w, lru_conv_b, lru_wa, lru_ba, lru_wx, lru_bx, lru_lambda, lru_w_out, attn_w_q, attn_w_k, attn_w_v, attn_w_o, attn_lambda_q1, attn_lambda_k1, attn_lambda_q2, attn_lambda_k2, attn_subln_g, rel_bias):
    n_seq, seq, d = x_prompt.shape
    n_batch, n_new, _ = x_sample.shape
    n_pool, page, _, _, _ = cache_k.shape
    n_pages = page_table.shape[1]
    depth = norm_g.shape[0]
    assert page == PAGE and n_new == SUBLANES and depth == 2 and lru_w_y.shape[0] == 1
    t_real = seq + N_META
    t_pad = -(-t_real // SEQ_TILE) * SEQ_TILE
    rows_p = n_seq * t_pad
    rows = rows_p + n_batch * n_new
    assert rows % ROW_TILE == 0 and rows_p % SEQ_TILE == 0 and (rows - rows_p) % SEQ_TILE == 0
    past_len = n_pages * PAGE

    meta = meta_tokens.astype(x_prompt.dtype)
    zpad = jnp.zeros((t_pad - t_real, d), x_prompt.dtype)
    parts = []
    for b in range(n_seq):
        parts += [meta, x_prompt[b], zpad]
    x = jnp.concatenate(parts + [x_sample.reshape(n_batch * n_new, d)], axis=0)

    def ffn(x, l, i):
        return _ffn(x, norm_g[l, 2 * i * 2:2 * i * 2 + 2],
                    ffn_w_gate[l, i].astype(bf16), ffn_w_up[l, i].astype(bf16),
                    ffn_w_down[l, i].astype(bf16))

    x = ffn(x, 0, 0)
    yb, u = _lru_in(x, norm_g[0, 2:3], lru_w_y[0].astype(bf16), lru_w_x[0].astype(bf16))
    cb_rows = jnp.pad(state_conv[0], ((0, 0), (SUBLANES - (CONV_W - 1), 0), (0, 0))).reshape(n_batch * n_new, d)
    h0_rows = jnp.repeat(state_h[0], n_new, axis=0)
    vecs = jnp.stack([lru_conv_b[0], lru_ba[0], lru_bx[0], lru_lambda[0]])
    hy, h_p, hs = _scan(u, yb, cb_rows, h0_rows, lru_conv_w[0], vecs,
                        lru_wa[0].astype(bf16), lru_wx[0].astype(bf16), n_seq, t_pad, t_real)
    x = _proj_res(hy, x, lru_w_out[0].astype(bf16), norm_g[0, 3:4])
    x = ffn(x, 0, 1)
    k, v, kb, vb = _kv(x, kv_norm_g[None], attn_w_k.astype(bf16), attn_w_v.astype(bf16))

    lam_init = 0.8 - 0.6 * math.exp(-0.3 * 1)
    x = ffn(x, 1, 0)
    q = _q_proj(x, norm_g[1, 2:3], attn_w_q[0].astype(bf16))
    g_tiles, bias_s = _bias_tables(rel_bias, past_len, n_new)
    bias_s = bias_s.reshape(2 * N_HEADS * n_new, past_len + PAGE)
    lamv = jnp.stack([attn_lambda_q1[0], attn_lambda_k1[0], attn_lambda_q2[0], attn_lambda_k2[0]])
    sg = attn_subln_g[0][None]
    o = _attn_prompt(q, kb, vb, g_tiles, rel_bias, lamv, sg, n_seq, t_pad, lam_init)
    cache_kt = jnp.transpose(cache_k, (0, 2, 3, 4, 1)).reshape(n_pool, d, PAGE)
    cache_vr = cache_v.reshape(n_pool, PAGE * N_HEADS, d // N_HEADS)
    o = _attn_sample(q, k, v, cache_kt, cache_vr, page_table, bias_s, lamv, sg, o, rows_p, n_new, lam_init)
    x = _proj_res(o, x, attn_w_o[0].astype(bf16), norm_g[1, 3:4])
    x = ffn(x, 1, 1)

    def prompt_rows(a):
        return a[:rows_p].reshape(n_seq, t_pad, d)

    def sample_rows(a):
        return a[rows_p:].reshape(n_batch, n_new, d)

    y_prompt = prompt_rows(x)[:, N_META:t_real]
    y_sample = sample_rows(x)
    k_p = prompt_rows(k)[:, :t_real].reshape(n_seq, t_real, N_HEADS, 2, HEAD_DIM)
    v_p = prompt_rows(v)[:, :t_real].reshape(n_seq, t_real, N_HEADS, 2 * HEAD_DIM)
    conv_p = prompt_rows(u)[:, t_real - (CONV_W - 1):t_real][None]
    h_p = h_p.reshape(1, n_seq, d)
    k_s = sample_rows(k).reshape(n_batch, n_new, N_HEADS, 2, HEAD_DIM)
    v_s = sample_rows(v).reshape(n_batch, n_new, N_HEADS, 2 * HEAD_DIM)
    conv_s = sample_rows(u)[:, n_new - (CONV_W - 1):][None]
    h_s = hs.reshape(n_batch, n_new, d)[:, n_new - 1][None]
    return (y_prompt, y_sample, k_p, v_p, conv_p, h_p, k_s, v_s, conv_s, h_s)
```

```python
import functools
import math

import numpy as np
import jax
import jax.numpy as jnp
from jax import lax
from jax.experimental import pallas as pl
from jax.experimental.pallas import tpu as pltpu

f32 = jnp.float32
bf16 = jnp.bfloat16

N_META = 16
N_HEADS = 8
HEAD_DIM = 64
CONV_W = 4
LRU_BLOCKS = 4
LRU_C = 8.0
N_BUCKETS = 32
MAX_DIST = 128
NORM_EPS = 1e-6
SUBLN_EPS = 1e-5
NEG_INF = -1e30
PAGE = 128

V7X_VMEM_BYTES = 64 * 1024 * 1024
VMEM_LIMIT = V7X_VMEM_BYTES - 8 * 1024 * 1024
SUBLANES = 8
LANES = 128

ROW_TILE = 512
SEQ_TILE = 256
ATT_TILE = 512


def _cparams(n_axes, flags=None):
    return pltpu.CompilerParams(dimension_semantics=("arbitrary",) * n_axes,
                                vmem_limit_bytes=VMEM_LIMIT, flags=flags)


def _const_spec(shape):
    nd = len(shape)
    return pl.BlockSpec(shape, lambda *_: (0,) * nd, pipeline_mode=pl.Buffered(1))


def _rms(x, g, eps):
    ms = jnp.mean(x * x, axis=-1, keepdims=True)
    return x * lax.rsqrt(ms + eps) * g


def _dot(a, b):
    return jnp.dot(a, b, preferred_element_type=f32)


def _dot_nt(a, b):
    return lax.dot_general(a, b, (((1,), (1,)), ((), ())), preferred_element_type=f32)


def _ffn_kernel(x_ref, g_ref, wg_ref, wu_ref, wd_ref, o_ref):
    x = x_ref[...]
    xn = _rms(x, g_ref[0:1, :], NORM_EPS).astype(bf16)
    hg = _dot(xn, wg_ref[...])
    hu = _dot(xn, wu_ref[...])
    act = (jax.nn.silu(hg) * hu).astype(bf16)
    y = _dot(act, wd_ref[...])
    o_ref[...] = x + 0.5 * _rms(y, g_ref[1:2, :], NORM_EPS)


def _ffn(x, g2, wg, wu, wd):
    rows, d = x.shape
    ffn = wg.shape[1]
    row = pl.BlockSpec((ROW_TILE, d), lambda i: (i, 0))
    return pl.pallas_call(
        _ffn_kernel,
        grid=(rows // ROW_TILE,),
        in_specs=[row, _const_spec((2, d)), _const_spec((d, ffn)), _const_spec((d, ffn)),
                  _const_spec((ffn, d))],
        out_specs=row,
        out_shape=jax.ShapeDtypeStruct((rows, d), f32),
        compiler_params=_cparams(1),
        name="ffn",
    )(x, g2, wg, wu, wd)


def _lru_in_kernel(x_ref, g_ref, wy_ref, wx_ref, yb_ref, u_ref):
    h = _rms(x_ref[...], g_ref[...], NORM_EPS).astype(bf16)
    yb_ref[...] = jax.nn.gelu(_dot(h, wy_ref[...]))
    u_ref[...] = _dot(h, wx_ref[...])


def _lru_in(x, g, wy, wx):
    rows, d = x.shape
    row = pl.BlockSpec((ROW_TILE, d), lambda i: (i, 0))
    return pl.pallas_call(
        _lru_in_kernel,
        grid=(rows // ROW_TILE,),
        in_specs=[row, _const_spec((1, d)), _const_spec((d, d)), _const_spec((d, d))],
        out_specs=[row, row],
        out_shape=[jax.ShapeDtypeStruct((rows, d), f32)] * 2,
        compiler_params=_cparams(1),
        name="lru_in",
    )(x, g, wy, wx)


def _proj_res_kernel(a_ref, x_ref, w_ref, g_ref, o_ref):
    m = _dot(a_ref[...].astype(bf16), w_ref[...])
    o_ref[...] = x_ref[...] + _rms(m, g_ref[...], NORM_EPS)


def _proj_res(a, x, w, g):
    rows, d = x.shape
    row = pl.BlockSpec((ROW_TILE, d), lambda i: (i, 0))
    return pl.pallas_call(
        _proj_res_kernel,
        grid=(rows // ROW_TILE,),
        in_specs=[row, row, _const_spec((d, d)), _const_spec((1, d))],
        out_specs=row,
        out_shape=jax.ShapeDtypeStruct((rows, d), f32),
        compiler_params=_cparams(1),
        name="proj_res",
    )(a, x, w, g)


def _kv_kernel(x_ref, g_ref, wk_ref, wv_ref, k_ref, v_ref, kb_ref, vb_ref):
    hk = _rms(x_ref[...], g_ref[...], NORM_EPS).astype(bf16)
    k = _dot(hk, wk_ref[...])
    v = _dot(hk, wv_ref[...])
    k_ref[...] = k
    v_ref[...] = v
    kb_ref[...] = k.astype(bf16)
    vb_ref[...] = v.astype(bf16)


def _kv(x, g, wk, wv):
    rows, d = x.shape
    row = pl.BlockSpec((ROW_TILE, d), lambda i: (i, 0))
    return pl.pallas_call(
        _kv_kernel,
        grid=(rows // ROW_TILE,),
        in_specs=[row, _const_spec((1, d)), _const_spec((d, d)), _const_spec((d, d))],
        out_specs=[row] * 4,
        out_shape=[jax.ShapeDtypeStruct((rows, d), f32)] * 2
        + [jax.ShapeDtypeStruct((rows, d), bf16)] * 2,
        compiler_params=_cparams(1),
        name="kv_proj",
    )(x, g, wk, wv)


def _q_kernel(x_ref, g_ref, wq_ref, q_ref):
    h = _rms(x_ref[...], g_ref[...], NORM_EPS).astype(bf16)
    q_ref[...] = _dot(h, wq_ref[...]) * (HEAD_DIM ** -0.5)


def _q_proj(x, g, wq):
    rows, d = x.shape
    row = pl.BlockSpec((ROW_TILE, d), lambda i: (i, 0))
    return pl.pallas_call(
        _q_kernel,
        grid=(rows // ROW_TILE,),
        in_specs=[row, _const_spec((1, d)), _const_spec((d, d))],
        out_specs=row,
        out_shape=jax.ShapeDtypeStruct((rows, d), f32),
        compiler_params=_cparams(1),
        name="q_proj",
    )(x, g, wq)


def _log_sigmoid(x):
    return -(jnp.maximum(-x, 0.0) + jnp.log1p(jnp.exp(-jnp.abs(x))))


def _scan_rows(a, b, t, shifts):
    for s in shifts:
        keep = t >= s
        a_s = jnp.where(keep, pltpu.roll(a, s, 0), 1.0)
        b_s = jnp.where(keep, pltpu.roll(b, s, 0), 0.0)
        b = a * b_s + b
        a = a * a_s
    return a, b


def _scan_kernel(n_seq_tiles, n_prompt_tiles, last_tile, last_row,
                 u_ref, yb_ref, cb_ref, h0_ref, cw_ref, vec_ref, wa_ref, wx_ref,
                 hy_ref, hp_ref, hs_ref, ucar_ref, hcar_ref):
    s = pl.program_id(0)
    rows, c = u_ref.shape
    bw = c // LRU_BLOCKS
    u = u_ref[...]
    row = lax.broadcasted_iota(jnp.int32, (rows, 1), 0)
    conv_b, ba, bx, lam = (vec_ref[i:i + 1, :] for i in range(4))

    def gates_and_ab(uc):
        ucb = uc.astype(bf16)
        rp = jnp.concatenate([_dot(ucb[:, n * bw:(n + 1) * bw], wa_ref[n]) for n in range(LRU_BLOCKS)], axis=1)
        ip = jnp.concatenate([_dot(ucb[:, n * bw:(n + 1) * bw], wx_ref[n]) for n in range(LRU_BLOCKS)], axis=1)
        r = jax.nn.sigmoid(rp + ba)
        i = jax.nn.sigmoid(ip + bx)
        log_a = LRU_C * r * _log_sigmoid(lam)
        a = jnp.exp(log_a)
        b = jnp.sqrt(-jnp.tanh(log_a) * (a * a + 1.0)) * (i * uc)
        return a, b

    @pl.when(s < n_prompt_tiles)
    def _prompt():
        i = jnp.where(s >= n_seq_tiles, s - n_seq_tiles, s)

        @pl.when(i == 0)
        def _():
            ucar_ref[...] = jnp.zeros_like(ucar_ref)
            hcar_ref[...] = jnp.zeros_like(hcar_ref)

        prev = ucar_ref[...]
        t8 = row[0:SUBLANES]
        uc = conv_b + cw_ref[CONV_W - 1:CONV_W, :] * u
        for j in range(1, CONV_W):
            uj = pltpu.roll(u, j, 0)
            head = jnp.where(t8 < j, pltpu.roll(prev, j, 0), uj[0:SUBLANES])
            uj = jnp.concatenate([head, uj[SUBLANES:]], axis=0)
            uc = uc + cw_ref[CONV_W - 1 - j:CONV_W - j, :] * uj
        a, b = gates_and_ab(uc)
        shifts = [1 << k for k in range(int(math.log2(rows)))]
        a, b = _scan_rows(a, b, row, shifts)
        h = a * hcar_ref[...] + b
        hy_ref[...] = (h * yb_ref[...]).astype(hy_ref.dtype)
        ucar_ref[...] = u[rows - SUBLANES:rows]
        hcar_ref[...] = h[rows - 1:rows]

        @pl.when(i == last_tile)
        def _():
            hp_ref[0] = h[last_row:last_row + 1]

    @pl.when(s >= n_prompt_tiles)
    def _sample():
        t = row % SUBLANES
        cb = cb_ref[...]
        uc = conv_b + cw_ref[CONV_W - 1:CONV_W, :] * u
        for j in range(1, CONV_W):
            uj = jnp.where(t >= j, pltpu.roll(u, j, 0), pltpu.roll(cb, rows - SUBLANES + j, 0))
            uc = uc + cw_ref[CONV_W - 1 - j:CONV_W - j, :] * uj
        a, b = gates_and_ab(uc)
        a, b = _scan_rows(a, b, t, [1, 2, 4])
        h = a * h0_ref[...] + b
        hy_ref[...] = (h * yb_ref[...]).astype(hy_ref.dtype)
        hs_ref[...] = h


def _scan(u, yb, cb_rows, h0_rows, conv_w, vecs, wa, wx, n_seq, t_pad, t_real):
    rows, c = u.shape
    n_seq_tiles = t_pad // SEQ_TILE
    n_prompt_tiles = n_seq * n_seq_tiles
    n_tiles = rows // SEQ_TILE
    bw = c // LRU_BLOCKS
    tile = pl.BlockSpec((SEQ_TILE, c), lambda s: (s, 0))
    samp = pl.BlockSpec((SEQ_TILE, c), lambda s: (jnp.maximum(s - n_prompt_tiles, 0), 0))
    kern = functools.partial(_scan_kernel, n_seq_tiles, n_prompt_tiles,
                             (t_real - 1) // SEQ_TILE, (t_real - 1) % SEQ_TILE)
    return pl.pallas_call(
        kern,
        grid=(n_tiles,),
        in_specs=[tile, tile, samp, samp, _const_spec((CONV_W, c)), _const_spec((4, c)),
                  _const_spec((LRU_BLOCKS, bw, bw)), _const_spec((LRU_BLOCKS, bw, bw))],
        out_specs=[tile,
                   pl.BlockSpec((1, 1, c), lambda s: (jnp.minimum(s // n_seq_tiles, n_seq - 1), 0, 0)),
                   samp],
        out_shape=[jax.ShapeDtypeStruct((rows, c), bf16),
                   jax.ShapeDtypeStruct((n_seq, 1, c), f32),
                   jax.ShapeDtypeStruct((rows - n_prompt_tiles * SEQ_TILE, c), f32)],
        scratch_shapes=[pltpu.VMEM((SUBLANES, c), f32), pltpu.VMEM((1, c), f32)],
        compiler_params=_cparams(1),
        name="lru_scan",
    )(u, yb, cb_rows, h0_rows, conv_w, vecs, wa, wx)


def _bucket_last_dist():
    n = np.arange(0, 4 * MAX_DIST, dtype=np.int32)
    max_exact = N_BUCKETS // 2
    nf = np.maximum(n, 1).astype(np.float32)
    large = max_exact + (np.log(nf / np.float32(max_exact)) / np.float32(math.log(MAX_DIST / max_exact))
                         * np.float32(N_BUCKETS - max_exact)).astype(np.int32)
    bucket = np.where(n < max_exact, n, np.minimum(large, N_BUCKETS - 1))
    assert np.all(np.diff(bucket) >= 0)
    last = [int(n[bucket == b].max()) if np.any(bucket == b) else None for b in range(N_BUCKETS)]
    assert last[N_BUCKETS - 1] == n[-1]
    return last


_BUCKET_LAST = _bucket_last_dist()
FAR_DIST = max(d for d in _BUCKET_LAST[:-1] if d is not None) + 1


def _bias_of_dist(dist, rb_ref, h):
    val = jnp.full(dist.shape, rb_ref[N_BUCKETS - 1, h], f32)
    for b in range(N_BUCKETS - 2, -1, -1):
        if _BUCKET_LAST[b] is not None:
            val = jnp.where(dist <= _BUCKET_LAST[b], rb_ref[b, h], val)
    return val


def _bias_kernel(past_len, rb_ref, g_ref, bs_ref):
    h = pl.program_id(0)
    _, nk, tq = g_ref.shape
    ki = lax.broadcasted_iota(jnp.int32, (nk, tq), 0)
    qi = lax.broadcasted_iota(jnp.int32, (nk, tq), 1)
    g_ref[0] = _bias_of_dist(qi - ki + (nk - tq), rb_ref, h)
    _, _, nq, wk = bs_ref.shape
    q = lax.broadcasted_iota(jnp.int32, (nq, wk), 0)
    col = lax.broadcasted_iota(jnp.int32, (nq, wk), 1)
    bias = _bias_of_dist(past_len + q - col, rb_ref, h)
    bs_ref[0, 0] = bias
    bs_ref[1, 0] = bias


def _bias_tables(rel_bias, past_len, n_new):
    g_shape = (N_HEADS, 2 * ATT_TILE, ATT_TILE)
    bs_shape = (2, N_HEADS, n_new, past_len + PAGE)
    return pl.pallas_call(
        functools.partial(_bias_kernel, past_len),
        grid=(N_HEADS,),
        in_specs=[pl.BlockSpec(memory_space=pltpu.SMEM)],
        out_specs=[pl.BlockSpec((1,) + g_shape[1:], lambda h: (h, 0, 0)),
                   pl.BlockSpec((2, 1) + bs_shape[2:], lambda h: (0, h, 0, 0))],
        out_shape=[jax.ShapeDtypeStruct(g_shape, f32), jax.ShapeDtypeStruct(bs_shape, f32)],
        compiler_params=_cparams(1),
        name="rel_bias_tables",
    )(rel_bias)


def _lambda(lamv_ref, lam_init):
    l1 = jnp.sum(lamv_ref[0:1, :] * lamv_ref[1:2, :], axis=-1, keepdims=True)
    l2 = jnp.sum(lamv_ref[2:3, :] * lamv_ref[3:4, :], axis=-1, keepdims=True)
    return jnp.exp(l1) - jnp.exp(l2) + lam_init


def _rows(start, n):
    if isinstance(start, int):
        return pl.ds(start, n)
    return pl.ds(pl.multiple_of(start, SEQ_TILE), n)


def _attn_prompt_kernel(lam_init, rb_ref, q_ref, k_ref, v_ref, g_ref, lamv_ref, sgc_ref, o_ref, vt_ref, s_ref):
    h = pl.program_id(1)
    t_pad, hw = q_ref.shape
    tk, tc = ATT_TILE, SEQ_TILE
    n_full, tail = divmod(t_pad, tk)
    far_bias = rb_ref[N_BUCKETS - 1, h]
    lam = _lambda(lamv_ref, lam_init)
    sgc = sgc_ref[...]

    def fill_vt(j, n):
        vt_ref[j, :, 0:n] = v_ref[_rows(j * tk, n), :].astype(f32).T.astype(bf16)

    def fill_body(j, carry):
        fill_vt(j, tk)
        return carry

    lax.fori_loop(0, n_full, fill_body, 0)
    if tail:
        fill_vt(n_full, tail)

    row = lax.broadcasted_iota(jnp.int32, (hw, tc), 0)

    def chain_q(q0, c):
        qt = q_ref[_rows(q0 + c * tc, tc), :].T
        return jnp.concatenate([jnp.where(row < HEAD_DIM, qt, 0.0), jnp.where(row >= HEAD_DIM, qt, 0.0)],
                               axis=1).astype(bf16)

    def update(state, s, off, vt):
        m, l, acc = state
        smax = jnp.max(s, axis=0, keepdims=True)
        m_new = jnp.maximum(m, smax if off is None else smax + off)
        alpha = jnp.exp(m - m_new)
        p = jnp.exp(s - (m_new if off is None else m_new - off))
        l = alpha * l + jnp.sum(p, axis=0, keepdims=True)
        acc = alpha * acc + _dot(vt, p.astype(bf16))
        return m_new, l, acc

    def attend(q0, n_chain, n_prev):
        qq = [chain_q(q0, c) for c in range(n_chain)]

        def logits(j, slot):
            kb = k_ref[_rows(j * tk, tk), :]
            for c in range(n_chain):
                s_ref[slot, c] = _dot(kb, qq[c])

        def far_step(j, slot, states):
            logits(j + 1, 1 - slot)
            vt = vt_ref[j]
            return tuple(update(states[c], s_ref[slot, c], far_bias, vt) for c in range(n_chain))

        def near_step(j, slot, states):
            vt = vt_ref[j]
            out = []
            for c in range(n_chain):
                g = g_ref[0, 0:tk, c * tc:(c + 1) * tc]
                out.append(update(states[c], s_ref[slot, c] + jnp.concatenate([g, g], axis=1), None, vt))
            return tuple(out)

        init = tuple((jnp.full((1, 2 * tc), NEG_INF, f32), jnp.zeros((1, 2 * tc), f32),
                      jnp.zeros((hw, 2 * tc), f32)) for _ in range(n_chain))
        n_far = jnp.maximum(n_prev - 1, 0)
        n_pair = n_far // 2
        odd = n_far - 2 * n_pair
        has_near = jnp.minimum(n_prev, 1)
        logits(0, 0)
        states = lax.fori_loop(
            0, n_pair, lambda i, st: far_step(2 * i + 1, 1, far_step(2 * i, 0, st)), init)
        states = lax.fori_loop(0, odd, lambda i, st: far_step(n_far - 1, 0, st), states)
        states = lax.fori_loop(0, has_near * (1 - odd), lambda i, st: near_step(n_far, 0, st), states)
        states = lax.fori_loop(0, has_near * odd, lambda i, st: near_step(n_far, 1, st), states)

        for c in range(n_chain):
            nk = (c + 1) * tc
            s = _dot(k_ref[_rows(q0, nk), :], qq[c])
            g = g_ref[0, tk:tk + nk, c * tc:(c + 1) * tc]
            ki = lax.broadcasted_iota(jnp.int32, (nk, 2 * tc), 0)
            qi = c * tc + (lax.broadcasted_iota(jnp.int32, (nk, 2 * tc), 1) & (tc - 1))
            s = jnp.where(ki <= qi, s + jnp.concatenate([g, g], axis=1), NEG_INF)
            m, l, acc = update(states[c], s, None, vt_ref[n_prev][:, 0:nk])
            on = acc * (1.0 / l)
            ot = on[:, 0:tc] - lam * on[:, tc:2 * tc]
            ms = jnp.mean(ot * ot, axis=0, keepdims=True)
            ot = ot * lax.rsqrt(ms + SUBLN_EPS) * sgc * (1.0 - lam_init)
            o_ref[_rows(q0 + c * tc, tc), :] = ot.T

    def tile_body(p, carry):
        attend(p * tk, tk // tc, p)
        return carry

    lax.fori_loop(0, n_full, tile_body, 0)
    if tail:
        attend(n_full * tk, tail // tc, n_full)


def _attn_prompt(q, kb, vb, g_tiles, rel_bias, lamv, subln_g, n_seq, t_pad, lam_init):
    rows, d = q.shape
    hw = d // N_HEADS
    assert ATT_TILE >= FAR_DIST and ATT_TILE % SEQ_TILE == 0 and t_pad % SEQ_TILE == 0
    blk = lambda b, h: (b, h)
    sgc = subln_g.reshape(hw, 1)
    return pl.pallas_call(
        functools.partial(_attn_prompt_kernel, lam_init),
        grid=(n_seq, N_HEADS),
        in_specs=[pl.BlockSpec(memory_space=pltpu.SMEM),
                  pl.BlockSpec((t_pad, hw), blk), pl.BlockSpec((t_pad, hw), blk),
                  pl.BlockSpec((t_pad, hw), blk),
                  pl.BlockSpec((1,) + g_tiles.shape[1:], lambda b, h: (h, 0, 0)),
                  _const_spec(lamv.shape), _const_spec(sgc.shape)],
        out_specs=pl.BlockSpec((t_pad, hw), blk),
        out_shape=jax.ShapeDtypeStruct((rows, d), f32),
        scratch_shapes=[pltpu.VMEM((pl.cdiv(t_pad, ATT_TILE), hw, ATT_TILE), bf16),
                        pltpu.VMEM((2, ATT_TILE // SEQ_TILE, ATT_TILE, 2 * SEQ_TILE), f32)],
        compiler_params=_cparams(2),
        name="attn_prompt",
    )(rel_bias, q, kb, vb, g_tiles, lamv, sgc)


def _attn_sample_kernel(lam_init, n_pages, pt_ref, q_ref, kn_ref, vn_ref, bias_ref, lamv_ref, sg_ref,
                        *refs):
    k_refs = refs[:n_pages]
    v_refs = refs[n_pages:2 * n_pages]
    o_ref = refs[2 * n_pages + 1]
    n_new, d = q_ref.shape
    n_grp = 2 * N_HEADS
    n_rows = n_grp * n_new
    hw = d // N_HEADS
    lam = _lambda(lamv_ref, lam_init)

    qt = jnp.concatenate([q_ref[...]] * n_grp, axis=0)
    r = lax.broadcasted_iota(jnp.int32, (n_rows, d), 0)
    col = lax.broadcasted_iota(jnp.int32, (n_rows, d), 1)
    row_grp = ((r // n_new) % N_HEADS) * 2 + r // (N_HEADS * n_new)
    wq = jnp.where(col // HEAD_DIM == row_grp, qt, 0.0).astype(bf16)

    pad = jnp.zeros((PAGE - n_new, d), f32)
    k_new = jnp.concatenate([kn_ref[...], pad], axis=0).astype(bf16)
    v_new = jnp.concatenate([vn_ref[...], pad], axis=0).astype(bf16)

    s = jnp.concatenate([_dot(wq, k_refs[j][0].astype(bf16)) for j in range(n_pages)]
                        + [_dot_nt(wq, k_new)], axis=1)
    s = s + bias_ref[...]
    past = n_pages * PAGE
    rr = lax.broadcasted_iota(jnp.int32, s.shape, 0)
    cc = lax.broadcasted_iota(jnp.int32, s.shape, 1)
    s = jnp.where(cc - past <= rr % n_new, s, NEG_INF)
    m = jnp.max(s, axis=-1, keepdims=True)
    p = jnp.exp(s - m)
    p = p / jnp.sum(p, axis=-1, keepdims=True)
    half = N_HEADS * n_new
    w = (p[0:half] - lam * p[half:2 * half]).astype(bf16)
    sg = sg_ref[...]
    heads = []
    for hh in range(N_HEADS):
        vh = jnp.concatenate([v_refs[j][0, pl.ds(hh, PAGE, stride=N_HEADS), :].astype(bf16)
                              for j in range(n_pages)] + [v_new[:, hh * hw:(hh + 1) * hw]], axis=0)
        out = _dot(w[hh * n_new:(hh + 1) * n_new, :], vh)
        heads.append(_rms(out, sg, SUBLN_EPS))
    o_ref[...] = jnp.concatenate(heads, axis=1) * (1.0 - lam_init)


def _attn_sample(q, k_new, v_new, cache_k, cache_v, page_table, bias_s, lamv, subln_g, o_all,
                 row0, n_new, lam_init):
    n_batch, n_pages = page_table.shape
    d = q.shape[1]
    blk0 = row0 // n_new
    tok = pl.BlockSpec((n_new, d), lambda b, pt: (blk0 + b, 0))

    def page_spec(j):
        return pl.BlockSpec((1,) + cache_k.shape[1:], lambda b, pt: (pt[b, j], 0, 0))

    grid_spec = pltpu.PrefetchScalarGridSpec(
        num_scalar_prefetch=1,
        grid=(n_batch,),
        in_specs=[tok, tok, tok,
                  pl.BlockSpec(bias_s.shape, lambda b, pt: (0, 0), pipeline_mode=pl.Buffered(1)),
                  pl.BlockSpec(lamv.shape, lambda b, pt: (0, 0), pipeline_mode=pl.Buffered(1)),
                  pl.BlockSpec(subln_g.shape, lambda b, pt: (0, 0), pipeline_mode=pl.Buffered(1))]
        + [page_spec(j) for j in range(n_pages)] * 2
        + [pl.BlockSpec(memory_space=pl.ANY)],
        out_specs=tok,
    )
    n_in = 1 + 6 + 2 * n_pages
    return pl.pallas_call(
        functools.partial(_attn_sample_kernel, lam_init, n_pages),
        grid_spec=grid_spec,
        out_shape=jax.ShapeDtypeStruct(o_all.shape, o_all.dtype),
        input_output_aliases={n_in: 0},
        compiler_params=_cparams(1),
        name="attn_sample",
    )(page_table, q, k_new, v_new, bias_s, lamv, subln_g,
      *([cache_k] * n_pages), *([cache_v] * n_pages), o_all)


def kernel(x_prompt, x_sample, cache_k, cache_v, page_table, state_conv, state_h, meta_tokens, norm_g, kv_norm_g, ffn_w_gate, ffn_w_up, ffn_w_down, lru_w_y, lru_w_x, lru_conv_w, lru_conv_b, lru_wa, lru_ba, lru_wx, lru_bx, lru_lambda, lru_w_out, attn_w_q, attn_w_k, attn_w_v, attn_w_o, attn_lambda_q1, attn_lambda_k1, attn_lambda_q2, attn_lambda_k2, attn_subln_g, rel_bias):
    n_seq, seq, d = x_prompt.shape
    n_batch, n_new, _ = x_sample.shape
    n_pool, page, _, _, _ = cache_k.shape
    n_pages = page_table.shape[1]
    depth = norm_g.shape[0]
    assert page == PAGE and n_new == SUBLANES and depth == 2 and lru_w_y.shape[0] == 1
    t_real = seq + N_META
    t_pad = -(-t_real // SEQ_TILE) * SEQ_TILE
    rows_p = n_seq * t_pad
    rows = rows_p + n_batch * n_new
    assert rows % ROW_TILE == 0 and rows_p % SEQ_TILE == 0 and (rows - rows_p) % SEQ_TILE == 0
    past_len = n_pages * PAGE

    meta = meta_tokens.astype(x_prompt.dtype)
    zpad = jnp.zeros((t_pad - t_real, d), x_prompt.dtype)
    parts = []
    for b in range(n_seq):
        parts += [meta, x_prompt[b], zpad]
    x = jnp.concatenate(parts + [x_sample.reshape(n_batch * n_new, d)], axis=0)

    def ffn(x, l, i):
        return _ffn(x, norm_g[l, 2 * i * 2:2 * i * 2 + 2],
                    ffn_w_gate[l, i].astype(bf16), ffn_w_up[l, i].astype(bf16),
                    ffn_w_down[l, i].astype(bf16))

    x = ffn(x, 0, 0)
    yb, u = _lru_in(x, norm_g[0, 2:3], lru_w_y[0].astype(bf16), lru_w_x[0].astype(bf16))
    cb_rows = jnp.pad(state_conv[0], ((0, 0), (SUBLANES - (CONV_W - 1), 0), (0, 0))).reshape(n_batch * n_new, d)
    h0_rows = jnp.repeat(state_h[0], n_new, axis=0)
    vecs = jnp.stack([lru_conv_b[0], lru_ba[0], lru_bx[0], lru_lambda[0]])
    hy, h_p, hs = _scan(u, yb, cb_rows, h0_rows, lru_conv_w[0], vecs,
                        lru_wa[0].astype(bf16), lru_wx[0].astype(bf16), n_seq, t_pad, t_real)
    x = _proj_res(hy, x, lru_w_out[0].astype(bf16), norm_g[0, 3:4])
    x = ffn(x, 0, 1)
    k, v, kb, vb = _kv(x, kv_norm_g[None], attn_w_k.astype(bf16), attn_w_v.astype(bf16))

    lam_init = 0.8 - 0.6 * math.exp(-0.3 * 1)
    x = ffn(x, 1, 0)
    q = _q_proj(x, norm_g[1, 2:3], attn_w_q[0].astype(bf16))
    g_tiles, bias_s = _bias_tables(rel_bias, past_len, n_new)
    bias_s = bias_s.reshape(2 * N_HEADS * n_new, past_len + PAGE)
    lamv = jnp.stack([attn_lambda_q1[0], attn_lambda_k1[0], attn_lambda_q2[0], attn_lambda_k2[0]])
    sg = attn_subln_g[0][None]
    o = _attn_prompt(q, kb, vb, g_tiles, rel_bias, lamv, sg, n_seq, t_pad, lam_init)
    cache_kt = jnp.transpose(cache_k, (0, 2, 3, 4, 1)).reshape(n_pool, d, PAGE)
    cache_vr = cache_v.reshape(n_pool, PAGE * N_HEADS, d // N_HEADS)
    o = _attn_sample(q, k, v, cache_kt, cache_vr, page_table, bias_s, lamv, sg, o, rows_p, n_new, lam_init)
    x = _proj_res(o, x, attn_w_o[0].astype(bf16), norm_g[1, 3:4])
    x = ffn(x, 1, 1)

    def prompt_rows(a):
        return a[:rows_p].reshape(n_seq, t_pad, d)

    def sample_rows(a):
        return a[rows_p:].reshape(n_batch, n_new, d)

    y_prompt = prompt_rows(x)[:, N_META:t_real]
    y_sample = sample_rows(x)
    k_p = prompt_rows(k)[:, :t_real].reshape(n_seq, t_real, N_HEADS, 2, HEAD_DIM)
    v_p = prompt_rows(v)[:, :t_real].reshape(n_seq, t_real, N_HEADS, 2 * HEAD_DIM)
    conv_p = prompt_rows(u)[:, t_real - (CONV_W - 1):t_real][None]
    h_p = h_p.reshape(1, n_seq, d)
    k_s = sample_rows(k).reshape(n_batch, n_new, N_HEADS, 2, HEAD_DIM)
    v_s = sample_rows(v).reshape(n_batch, n_new, N_HEADS, 2 * HEAD_DIM)
    conv_s = sample_rows(u)[:, n_new - (CONV_W - 1):][None]
    h_s = hs.reshape(n_batch, n_new, d)[:, n_new - 1][None]
    return (y_prompt, y_sample, k_p, v_p, conv_p, h_p, k_s, v_s, conv_s, h_s)
```

```python
import functools
import math

import numpy as np
import jax
import jax.numpy as jnp
from jax import lax
from jax.experimental import pallas as pl
from jax.experimental.pallas import tpu as pltpu

f32 = jnp.float32
bf16 = jnp.bfloat16

N_META = 16
N_HEADS = 8
HEAD_DIM = 64
CONV_W = 4
LRU_BLOCKS = 4
LRU_C = 8.0
N_BUCKETS = 32
MAX_DIST = 128
NORM_EPS = 1e-6
SUBLN_EPS = 1e-5
NEG_INF = -1e30
PAGE = 128

V7X_VMEM_BYTES = 64 * 1024 * 1024
VMEM_LIMIT = V7X_VMEM_BYTES - 8 * 1024 * 1024
SUBLANES = 8
LANES = 128

ROW_TILE = 512
SEQ_TILE = 256
ATT_TILE = 512
ONES_ROWS = 16


def _cparams(n_axes, flags=None):
    return pltpu.CompilerParams(dimension_semantics=("arbitrary",) * n_axes,
                                vmem_limit_bytes=VMEM_LIMIT, flags=flags)


def _const_spec(shape):
    nd = len(shape)
    return pl.BlockSpec(shape, lambda *_: (0,) * nd, pipeline_mode=pl.Buffered(1))


def _rms(x, g, eps):
    ms = jnp.mean(x * x, axis=-1, keepdims=True)
    return x * lax.rsqrt(ms + eps) * g


def _dot(a, b):
    return jnp.dot(a, b, preferred_element_type=f32)


def _dot_nt(a, b):
    return lax.dot_general(a, b, (((1,), (1,)), ((), ())), preferred_element_type=f32)


def _ffn_kernel(x_ref, g_ref, wg_ref, wu_ref, wd_ref, o_ref):
    x = x_ref[...]
    xn = _rms(x, g_ref[0:1, :], NORM_EPS).astype(bf16)
    hg = _dot(xn, wg_ref[...])
    hu = _dot(xn, wu_ref[...])
    act = (jax.nn.silu(hg) * hu).astype(bf16)
    y = _dot(act, wd_ref[...])
    o_ref[...] = x + 0.5 * _rms(y, g_ref[1:2, :], NORM_EPS)


def _ffn(x, g2, wg, wu, wd):
    rows, d = x.shape
    ffn = wg.shape[1]
    row = pl.BlockSpec((ROW_TILE, d), lambda i: (i, 0))
    return pl.pallas_call(
        _ffn_kernel,
        grid=(rows // ROW_TILE,),
        in_specs=[row, _const_spec((2, d)), _const_spec((d, ffn)), _const_spec((d, ffn)),
                  _const_spec((ffn, d))],
        out_specs=row,
        out_shape=jax.ShapeDtypeStruct((rows, d), f32),
        compiler_params=_cparams(1),
        name="ffn",
    )(x, g2, wg, wu, wd)


def _lru_in_kernel(x_ref, g_ref, wy_ref, wx_ref, yb_ref, u_ref):
    h = _rms(x_ref[...], g_ref[...], NORM_EPS).astype(bf16)
    yb_ref[...] = jax.nn.gelu(_dot(h, wy_ref[...]))
    u_ref[...] = _dot(h, wx_ref[...])


def _lru_in(x, g, wy, wx):
    rows, d = x.shape
    row = pl.BlockSpec((ROW_TILE, d), lambda i: (i, 0))
    return pl.pallas_call(
        _lru_in_kernel,
        grid=(rows // ROW_TILE,),
        in_specs=[row, _const_spec((1, d)), _const_spec((d, d)), _const_spec((d, d))],
        out_specs=[row, row],
        out_shape=[jax.ShapeDtypeStruct((rows, d), f32)] * 2,
        compiler_params=_cparams(1),
        name="lru_in",
    )(x, g, wy, wx)


def _proj_res_kernel(a_ref, x_ref, w_ref, g_ref, o_ref):
    m = _dot(a_ref[...].astype(bf16), w_ref[...])
    o_ref[...] = x_ref[...] + _rms(m, g_ref[...], NORM_EPS)


def _proj_res(a, x, w, g):
    rows, d = x.shape
    row = pl.BlockSpec((ROW_TILE, d), lambda i: (i, 0))
    return pl.pallas_call(
        _proj_res_kernel,
        grid=(rows // ROW_TILE,),
        in_specs=[row, row, _const_spec((d, d)), _const_spec((1, d))],
        out_specs=row,
        out_shape=jax.ShapeDtypeStruct((rows, d), f32),
        compiler_params=_cparams(1),
        name="proj_res",
    )(a, x, w, g)


def _kv_kernel(n_prompt_tiles, x_ref, g_ref, wk_ref, wv_ref, kp_ref, vp_ref, ks_ref, vs_ref, kb_ref, vb_ref):
    s = pl.program_id(0)
    hk = _rms(x_ref[...], g_ref[...], NORM_EPS).astype(bf16)
    k = _dot(hk, wk_ref[...])
    v = _dot(hk, wv_ref[...])
    kb_ref[...] = k.astype(bf16)
    vb_ref[...] = v.astype(bf16)

    @pl.when(s < n_prompt_tiles)
    def _prompt():
        kp_ref[0] = k
        vp_ref[0] = v

    @pl.when(s >= n_prompt_tiles)
    def _sample():
        ks_ref[...] = k
        vs_ref[...] = v


def _kv(x, g, wk, wv, n_seq, t_pad, t_real):
    rows, d = x.shape
    n_seq_tiles = t_pad // SEQ_TILE
    n_prompt_tiles = n_seq * n_seq_tiles
    rows_s = rows - n_prompt_tiles * SEQ_TILE
    tile = pl.BlockSpec((SEQ_TILE, d), lambda s: (s, 0))
    prompt = pl.BlockSpec((1, SEQ_TILE, d),
                          lambda s: (jnp.minimum(s // n_seq_tiles, n_seq - 1),
                                     jnp.where(s < n_prompt_tiles, s % n_seq_tiles, n_seq_tiles - 1), 0))
    samp = pl.BlockSpec((SEQ_TILE, d), lambda s: (jnp.maximum(s - n_prompt_tiles, 0), 0))
    return pl.pallas_call(
        functools.partial(_kv_kernel, n_prompt_tiles),
        grid=(rows // SEQ_TILE,),
        in_specs=[tile, _const_spec((1, d)), _const_spec((d, d)), _const_spec((d, d))],
        out_specs=[prompt, prompt, samp, samp, tile, tile],
        out_shape=[jax.ShapeDtypeStruct((n_seq, t_real, d), f32)] * 2
        + [jax.ShapeDtypeStruct((rows_s, d), f32)] * 2
        + [jax.ShapeDtypeStruct((rows, d), bf16)] * 2,
        compiler_params=_cparams(1),
        name="kv_proj",
    )(x, g, wk, wv)


def _q_kernel(x_ref, g_ref, wq_ref, q_ref):
    h = _rms(x_ref[...], g_ref[...], NORM_EPS).astype(bf16)
    q_ref[...] = _dot(h, wq_ref[...]) * (HEAD_DIM ** -0.5)


def _q_proj(x, g, wq):
    rows, d = x.shape
    row = pl.BlockSpec((ROW_TILE, d), lambda i: (i, 0))
    return pl.pallas_call(
        _q_kernel,
        grid=(rows // ROW_TILE,),
        in_specs=[row, _const_spec((1, d)), _const_spec((d, d))],
        out_specs=row,
        out_shape=jax.ShapeDtypeStruct((rows, d), f32),
        compiler_params=_cparams(1),
        name="q_proj",
    )(x, g, wq)


def _log_sigmoid(x):
    return -(jnp.maximum(-x, 0.0) + jnp.log1p(jnp.exp(-jnp.abs(x))))


def _scan_groups(a, b):
    rows, c = a.shape
    a = a.reshape(rows // SUBLANES, SUBLANES, c)
    b = b.reshape(rows // SUBLANES, SUBLANES, c)
    t = lax.broadcasted_iota(jnp.int32, (1, SUBLANES, 1), 1)
    s = 1
    while s < SUBLANES:
        keep = t >= s
        a_s = jnp.where(keep, pltpu.roll(a, s, 1), 1.0)
        b_s = jnp.where(keep, pltpu.roll(b, s, 1), 0.0)
        b = a * b_s + b
        a = a * a_s
        s *= 2
    return a.reshape(rows, c), b.reshape(rows, c)


def _scan_kernel(n_seq_tiles, n_prompt_tiles, last_tile, last_row,
                 u_ref, yb_ref, cb_ref, h0_ref, cw_ref, vec_ref, wa_ref, wx_ref,
                 hy_ref, hp_ref, hs_ref, uext_ref, hcar_ref):
    s = pl.program_id(0)
    rows, c = u_ref.shape
    bw = c // LRU_BLOCKS
    u = u_ref[...]
    row = lax.broadcasted_iota(jnp.int32, (rows, 1), 0)
    conv_b, ba, bx, lam = (vec_ref[i:i + 1, :] for i in range(4))

    def gates_and_ab(uc):
        ucb = uc.astype(bf16)
        rp = jnp.concatenate([_dot(ucb[:, n * bw:(n + 1) * bw], wa_ref[n]) for n in range(LRU_BLOCKS)], axis=1)
        ip = jnp.concatenate([_dot(ucb[:, n * bw:(n + 1) * bw], wx_ref[n]) for n in range(LRU_BLOCKS)], axis=1)
        r = jax.nn.sigmoid(rp + ba)
        i = jax.nn.sigmoid(ip + bx)
        log_a = LRU_C * r * _log_sigmoid(lam)
        a = jnp.exp(log_a)
        b = jnp.sqrt(-jnp.tanh(log_a) * (a * a + 1.0)) * (i * uc)
        return a, b

    @pl.when(s < n_prompt_tiles)
    def _prompt():
        i = jnp.where(s >= n_seq_tiles, s - n_seq_tiles, s)

        @pl.when(i == 0)
        def _():
            uext_ref[0:SUBLANES, :] = jnp.zeros((SUBLANES, c), f32)
            hcar_ref[...] = jnp.zeros_like(hcar_ref)

        uext_ref[SUBLANES:SUBLANES + rows, :] = u
        uc = conv_b + cw_ref[CONV_W - 1:CONV_W, :] * u
        for j in range(1, CONV_W):
            uc = uc + cw_ref[CONV_W - 1 - j:CONV_W - j, :] * uext_ref[SUBLANES - j:SUBLANES - j + rows, :]
        a, b = gates_and_ab(uc)
        a, b = _scan_groups(a, b)
        carry = hcar_ref[...]
        groups = []
        for g in range(rows // SUBLANES):
            hg = a[g * SUBLANES:(g + 1) * SUBLANES] * carry + b[g * SUBLANES:(g + 1) * SUBLANES]
            carry = hg[SUBLANES - 1:SUBLANES]
            groups.append(hg)
        h = jnp.concatenate(groups, axis=0)
        hy_ref[...] = (h * yb_ref[...]).astype(hy_ref.dtype)
        uext_ref[0:SUBLANES, :] = u[rows - SUBLANES:rows]
        hcar_ref[...] = carry

        @pl.when(i == last_tile)
        def _():
            hp_ref[0] = h[last_row:last_row + 1]

    @pl.when(s >= n_prompt_tiles)
    def _sample():
        t = row % SUBLANES
        cb = cb_ref[...]
        uc = conv_b + cw_ref[CONV_W - 1:CONV_W, :] * u
        for j in range(1, CONV_W):
            uj = jnp.where(t >= j, pltpu.roll(u, j, 0), pltpu.roll(cb, rows - SUBLANES + j, 0))
            uc = uc + cw_ref[CONV_W - 1 - j:CONV_W - j, :] * uj
        a, b = gates_and_ab(uc)
        a, b = _scan_groups(a, b)
        h = a * h0_ref[...] + b
        hy_ref[...] = (h * yb_ref[...]).astype(hy_ref.dtype)
        hs_ref[...] = h


def _scan(u, yb, cb_rows, h0_rows, conv_w, vecs, wa, wx, n_seq, t_pad, t_real):
    rows, c = u.shape
    n_seq_tiles = t_pad // SEQ_TILE
    n_prompt_tiles = n_seq * n_seq_tiles
    n_tiles = rows // SEQ_TILE
    bw = c // LRU_BLOCKS
    tile = pl.BlockSpec((SEQ_TILE, c), lambda s: (s, 0))
    samp = pl.BlockSpec((SEQ_TILE, c), lambda s: (jnp.maximum(s - n_prompt_tiles, 0), 0))
    kern = functools.partial(_scan_kernel, n_seq_tiles, n_prompt_tiles,
                             (t_real - 1) // SEQ_TILE, (t_real - 1) % SEQ_TILE)
    return pl.pallas_call(
        kern,
        grid=(n_tiles,),
        in_specs=[tile, tile, samp, samp, _const_spec((CONV_W, c)), _const_spec((4, c)),
                  _const_spec((LRU_BLOCKS, bw, bw)), _const_spec((LRU_BLOCKS, bw, bw))],
        out_specs=[tile,
                   pl.BlockSpec((1, 1, c), lambda s: (jnp.minimum(s // n_seq_tiles, n_seq - 1), 0, 0)),
                   samp],
        out_shape=[jax.ShapeDtypeStruct((rows, c), bf16),
                   jax.ShapeDtypeStruct((n_seq, 1, c), f32),
                   jax.ShapeDtypeStruct((rows - n_prompt_tiles * SEQ_TILE, c), f32)],
        scratch_shapes=[pltpu.VMEM((SUBLANES + SEQ_TILE, c), f32), pltpu.VMEM((1, c), f32)],
        compiler_params=_cparams(1),
        name="lru_scan",
    )(u, yb, cb_rows, h0_rows, conv_w, vecs, wa, wx)


def _bucket_last_dist():
    n = np.arange(0, 4 * MAX_DIST, dtype=np.int32)
    max_exact = N_BUCKETS // 2
    nf = np.maximum(n, 1).astype(np.float32)
    large = max_exact + (np.log(nf / np.float32(max_exact)) / np.float32(math.log(MAX_DIST / max_exact))
                         * np.float32(N_BUCKETS - max_exact)).astype(np.int32)
    bucket = np.where(n < max_exact, n, np.minimum(large, N_BUCKETS - 1))
    assert np.all(np.diff(bucket) >= 0)
    last = [int(n[bucket == b].max()) if np.any(bucket == b) else None for b in range(N_BUCKETS)]
    assert last[N_BUCKETS - 1] == n[-1]
    return last


_BUCKET_LAST = _bucket_last_dist()
FAR_DIST = max(d for d in _BUCKET_LAST[:-1] if d is not None) + 1


def _bias_of_dist(dist, rb_ref, h):
    val = jnp.full(dist.shape, rb_ref[N_BUCKETS - 1, h], f32)
    for b in range(N_BUCKETS - 2, -1, -1):
        if _BUCKET_LAST[b] is not None:
            val = jnp.where(dist <= _BUCKET_LAST[b], rb_ref[b, h], val)
    return val


def _bias_kernel(past_len, rb_ref, g_ref, bs_ref):
    h = pl.program_id(0)
    _, nk, tq = g_ref.shape
    ki = lax.broadcasted_iota(jnp.int32, (nk, tq), 0)
    qi = lax.broadcasted_iota(jnp.int32, (nk, tq), 1)
    g_ref[0] = _bias_of_dist(qi - ki + (nk - tq), rb_ref, h)
    _, _, nq, wk = bs_ref.shape
    q = lax.broadcasted_iota(jnp.int32, (nq, wk), 0)
    col = lax.broadcasted_iota(jnp.int32, (nq, wk), 1)
    bias = _bias_of_dist(past_len + q - col, rb_ref, h)
    bs_ref[0, 0] = bias
    bs_ref[1, 0] = bias


def _bias_tables(rel_bias, past_len, n_new):
    g_shape = (N_HEADS, 2 * ATT_TILE, ATT_TILE)
    bs_shape = (2, N_HEADS, n_new, past_len + PAGE)
    return pl.pallas_call(
        functools.partial(_bias_kernel, past_len),
        grid=(N_HEADS,),
        in_specs=[pl.BlockSpec(memory_space=pltpu.SMEM)],
        out_specs=[pl.BlockSpec((1,) + g_shape[1:], lambda h: (h, 0, 0)),
                   pl.BlockSpec((2, 1) + bs_shape[2:], lambda h: (0, h, 0, 0))],
        out_shape=[jax.ShapeDtypeStruct(g_shape, f32), jax.ShapeDtypeStruct(bs_shape, f32)],
        compiler_params=_cparams(1),
        name="rel_bias_tables",
    )(rel_bias)


def _lambda(lamv_ref, lam_init):
    l1 = jnp.sum(lamv_ref[0:1, :] * lamv_ref[1:2, :], axis=-1, keepdims=True)
    l2 = jnp.sum(lamv_ref[2:3, :] * lamv_ref[3:4, :], axis=-1, keepdims=True)
    return jnp.exp(l1) - jnp.exp(l2) + lam_init


def _rows(start, n):
    if isinstance(start, int):
        return pl.ds(start, n)
    return pl.ds(pl.multiple_of(start, SEQ_TILE), n)


def _attn_prompt_kernel(lam_init, rb_ref, q_ref, k_ref, v_ref, g_ref, lamv_ref, sgc_ref, o_ref, vt_ref, s_ref):
    h = pl.program_id(1)
    t_pad, hw = q_ref.shape
    tk, tc = ATT_TILE, SEQ_TILE
    n_full, tail = divmod(t_pad, tk)
    far_bias = rb_ref[N_BUCKETS - 1, h]
    lam = _lambda(lamv_ref, lam_init)
    sgc = sgc_ref[...]

    ones_rows = (lax.broadcasted_iota(jnp.int32, (ONES_ROWS, tk), 0) == 0).astype(bf16)

    def fill_vt(j, n):
        vt_ref[j, 0:hw, 0:n] = v_ref[_rows(j * tk, n), :].astype(f32).T.astype(bf16)
        vt_ref[j, hw:hw + ONES_ROWS, :] = ones_rows

    def fill_body(j, carry):
        fill_vt(j, tk)
        return carry

    lax.fori_loop(0, n_full, fill_body, 0)
    if tail:
        fill_vt(n_full, tail)

    row = lax.broadcasted_iota(jnp.int32, (hw, tc), 0)

    def chain_q(q0, c):
        qt = q_ref[_rows(q0 + c * tc, tc), :].T
        return jnp.concatenate([jnp.where(row < HEAD_DIM, qt, 0.0), jnp.where(row >= HEAD_DIM, qt, 0.0)],
                               axis=1).astype(bf16)

    def update(state, s, off, vt):
        m, acc = state
        smax = jnp.max(s, axis=0, keepdims=True)
        m_new = jnp.maximum(m, smax if off is None else smax + off)
        alpha = jnp.exp(m - m_new)
        p = jnp.exp(s - (m_new if off is None else m_new - off))
        acc = alpha * acc + _dot(vt, p.astype(bf16))
        return m_new, acc

    def attend(q0, n_chain, n_prev):
        qq = [chain_q(q0, c) for c in range(n_chain)]

        def logits(j, slot):
            kb = k_ref[_rows(j * tk, tk), :]
            for c in range(n_chain):
                s_ref[slot, c] = _dot(kb, qq[c])

        def far_step(j, slot, states):
            logits(j + 1, 1 - slot)
            vt = vt_ref[j]
            return tuple(update(states[c], s_ref[slot, c], far_bias, vt) for c in range(n_chain))

        def near_step(j, slot, states):
            vt = vt_ref[j]
            out = []
            for c in range(n_chain):
                g = g_ref[0, 0:tk, c * tc:(c + 1) * tc]
                out.append(update(states[c], s_ref[slot, c] + jnp.concatenate([g, g], axis=1), None, vt))
            return tuple(out)

        init = tuple((jnp.full((1, 2 * tc), NEG_INF, f32), jnp.zeros((hw + ONES_ROWS, 2 * tc), f32))
                     for _ in range(n_chain))
        n_far = jnp.maximum(n_prev - 1, 0)
        n_quad = n_far >> 2
        pair = (n_far >> 1) & 1
        odd = n_far & 1
        has_near = jnp.minimum(n_prev, 1)
        logits(0, 0)

        def far_pair(j, st):
            return far_step(j + 1, 1, far_step(j, 0, st))

        states = lax.fori_loop(0, n_quad, lambda i, st: far_pair(4 * i + 2, far_pair(4 * i, st)), init)
        states = lax.fori_loop(0, pair, lambda i, st: far_pair(4 * n_quad, st), states)
        states = lax.fori_loop(0, odd, lambda i, st: far_step(n_far - 1, 0, st), states)
        states = lax.fori_loop(0, has_near * (1 - odd), lambda i, st: near_step(n_far, 0, st), states)
        states = lax.fori_loop(0, has_near * odd, lambda i, st: near_step(n_far, 1, st), states)

        for c in range(n_chain):
            nk = (c + 1) * tc
            s = _dot(k_ref[_rows(q0, nk), :], qq[c])
            g = g_ref[0, tk:tk + nk, c * tc:(c + 1) * tc]
            ki = lax.broadcasted_iota(jnp.int32, (nk, 2 * tc), 0)
            qi = c * tc + (lax.broadcasted_iota(jnp.int32, (nk, 2 * tc), 1) & (tc - 1))
            s = jnp.where(ki <= qi, s + jnp.concatenate([g, g], axis=1), NEG_INF)
            _, acc = update(states[c], s, None, vt_ref[n_prev][:, 0:nk])
            on = acc[0:hw] * (1.0 / acc[hw:hw + 1])
            ot = on[:, 0:tc] - lam * on[:, tc:2 * tc]
            ms = jnp.mean(ot * ot, axis=0, keepdims=True)
            ot = ot * lax.rsqrt(ms + SUBLN_EPS) * sgc * (1.0 - lam_init)
            o_ref[_rows(q0 + c * tc, tc), :] = ot.T

    def tile_body(p, carry):
        attend(p * tk, tk // tc, p)
        return carry

    lax.fori_loop(0, n_full, tile_body, 0)
    if tail:
        attend(n_full * tk, tail // tc, n_full)


def _attn_prompt(q, kb, vb, g_tiles, rel_bias, lamv, subln_g, n_seq, t_pad, lam_init):
    rows, d = q.shape
    hw = d // N_HEADS
    assert ATT_TILE >= FAR_DIST and ATT_TILE % SEQ_TILE == 0 and t_pad % SEQ_TILE == 0
    blk = lambda b, h: (b, h)
    sgc = subln_g.reshape(hw, 1)
    return pl.pallas_call(
        functools.partial(_attn_prompt_kernel, lam_init),
        grid=(n_seq, N_HEADS),
        in_specs=[pl.BlockSpec(memory_space=pltpu.SMEM),
                  pl.BlockSpec((t_pad, hw), blk), pl.BlockSpec((t_pad, hw), blk),
                  pl.BlockSpec((t_pad, hw), blk),
                  pl.BlockSpec((1,) + g_tiles.shape[1:], lambda b, h: (h, 0, 0)),
                  _const_spec(lamv.shape), _const_spec(sgc.shape)],
        out_specs=pl.BlockSpec((t_pad, hw), blk),
        out_shape=jax.ShapeDtypeStruct((rows, d), f32),
        scratch_shapes=[pltpu.VMEM((pl.cdiv(t_pad, ATT_TILE), hw + ONES_ROWS, ATT_TILE), bf16),
                        pltpu.VMEM((2, ATT_TILE // SEQ_TILE, ATT_TILE, 2 * SEQ_TILE), f32)],
        compiler_params=_cparams(2),
        name="attn_prompt",
    )(rel_bias, q, kb, vb, g_tiles, lamv, sgc)


def _attn_sample_kernel(lam_init, n_pages, pt_ref, q_ref, kn_ref, vn_ref, bias_ref, lamv_ref, sg_ref,
                        *refs):
    k_refs = refs[:n_pages]
    v_refs = refs[n_pages:2 * n_pages]
    o_ref = refs[2 * n_pages + 1]
    n_new, d = q_ref.shape
    n_grp = 2 * N_HEADS
    n_rows = n_grp * n_new
    hw = d // N_HEADS
    lam = _lambda(lamv_ref, lam_init)

    qt = jnp.concatenate([q_ref[...]] * n_grp, axis=0)
    r = lax.broadcasted_iota(jnp.int32, (n_rows, d), 0)
    col = lax.broadcasted_iota(jnp.int32, (n_rows, d), 1)
    row_grp = ((r // n_new) % N_HEADS) * 2 + r // (N_HEADS * n_new)
    wq = jnp.where(col // HEAD_DIM == row_grp, qt, 0.0).astype(bf16)

    pad = jnp.zeros((PAGE - n_new, d), f32)
    k_new = jnp.concatenate([kn_ref[...], pad], axis=0).astype(bf16)
    v_new = jnp.concatenate([vn_ref[...], pad], axis=0).astype(bf16)

    s = jnp.concatenate([_dot(wq, k_refs[j][0].astype(bf16)) for j in range(n_pages)]
                        + [_dot_nt(wq, k_new)], axis=1)
    s = s + bias_ref[...]
    past = n_pages * PAGE
    rr = lax.broadcasted_iota(jnp.int32, s.shape, 0)
    cc = lax.broadcasted_iota(jnp.int32, s.shape, 1)
    s = jnp.where(cc - past <= rr % n_new, s, NEG_INF)
    m = jnp.max(s, axis=-1, keepdims=True)
    p = jnp.exp(s - m)
    p = p / jnp.sum(p, axis=-1, keepdims=True)
    half = N_HEADS * n_new
    w = (p[0:half] - lam * p[half:2 * half]).astype(bf16)
    sg = sg_ref[...]
    heads = []
    for hh in range(N_HEADS):
        vh = jnp.concatenate([v_refs[j][0, pl.ds(hh, PAGE, stride=N_HEADS), :].astype(bf16)
                              for j in range(n_pages)] + [v_new[:, hh * hw:(hh + 1) * hw]], axis=0)
        out = _dot(w[hh * n_new:(hh + 1) * n_new, :], vh)
        heads.append(_rms(out, sg, SUBLN_EPS))
    o_ref[...] = jnp.concatenate(heads, axis=1) * (1.0 - lam_init)


def _attn_sample(q, k_new, v_new, cache_k, cache_v, page_table, bias_s, lamv, subln_g, o_all,
                 row0, n_new, lam_init):
    n_batch, n_pages = page_table.shape
    d = q.shape[1]
    blk0 = row0 // n_new
    tok = pl.BlockSpec((n_new, d), lambda b, pt: (blk0 + b, 0))
    new = pl.BlockSpec((n_new, d), lambda b, pt: (b, 0))

    def page_spec(j):
        return pl.BlockSpec((1,) + cache_k.shape[1:], lambda b, pt: (pt[b, j], 0, 0))

    grid_spec = pltpu.PrefetchScalarGridSpec(
        num_scalar_prefetch=1,
        grid=(n_batch,),
        in_specs=[tok, new, new,
                  pl.BlockSpec(bias_s.shape, lambda b, pt: (0, 0), pipeline_mode=pl.Buffered(1)),
                  pl.BlockSpec(lamv.shape, lambda b, pt: (0, 0), pipeline_mode=pl.Buffered(1)),
                  pl.BlockSpec(subln_g.shape, lambda b, pt: (0, 0), pipeline_mode=pl.Buffered(1))]
        + [page_spec(j) for j in range(n_pages)] * 2
        + [pl.BlockSpec(memory_space=pl.ANY)],
        out_specs=tok,
    )
    n_in = 1 + 6 + 2 * n_pages
    return pl.pallas_call(
        functools.partial(_attn_sample_kernel, lam_init, n_pages),
        grid_spec=grid_spec,
        out_shape=jax.ShapeDtypeStruct(o_all.shape, o_all.dtype),
        input_output_aliases={n_in: 0},
        compiler_params=_cparams(1),
        name="attn_sample",
    )(page_table, q, k_new, v_new, bias_s, lamv, subln_g,
      *([cache_k] * n_pages), *([cache_v] * n_pages), o_all)


def kernel(x_prompt, x_sample, cache_k, cache_v, page_table, state_conv, state_h, meta_tokens, norm_g, kv_norm_g, ffn_w_gate, ffn_w_up, ffn_w_down, lru_w_y, lru_w_x, lru_conv_w, lru_conv_b, lru_wa, lru_ba, lru_wx, lru_bx, lru_lambda, lru_w_out, attn_w_q, attn_w_k, attn_w_v, attn_w_o, attn_lambda_q1, attn_lambda_k1, attn_lambda_q2, attn_lambda_k2, attn_subln_g, rel_bias):
    n_seq, seq, d = x_prompt.shape
    n_batch, n_new, _ = x_sample.shape
    n_pool, page, _, _, _ = cache_k.shape
    n_pages = page_table.shape[1]
    depth = norm_g.shape[0]
    assert page == PAGE and n_new == SUBLANES and depth == 2 and lru_w_y.shape[0] == 1
    t_real = seq + N_META
    t_pad = -(-t_real // SEQ_TILE) * SEQ_TILE
    rows_p = n_seq * t_pad
    rows = rows_p + n_batch * n_new
    assert rows % ROW_TILE == 0 and rows_p % SEQ_TILE == 0 and (rows - rows_p) % SEQ_TILE == 0
    past_len = n_pages * PAGE

    meta = meta_tokens.astype(x_prompt.dtype)
    zpad = jnp.zeros((t_pad - t_real, d), x_prompt.dtype)
    parts = []
    for b in range(n_seq):
        parts += [meta, x_prompt[b], zpad]
    x = jnp.concatenate(parts + [x_sample.reshape(n_batch * n_new, d)], axis=0)

    def ffn(x, l, i):
        return _ffn(x, norm_g[l, 2 * i * 2:2 * i * 2 + 2],
                    ffn_w_gate[l, i].astype(bf16), ffn_w_up[l, i].astype(bf16),
                    ffn_w_down[l, i].astype(bf16))

    x = ffn(x, 0, 0)
    yb, u = _lru_in(x, norm_g[0, 2:3], lru_w_y[0].astype(bf16), lru_w_x[0].astype(bf16))
    cb_rows = jnp.pad(state_conv[0], ((0, 0), (SUBLANES - (CONV_W - 1), 0), (0, 0))).reshape(n_batch * n_new, d)
    h0_rows = jnp.repeat(state_h[0], n_new, axis=0)
    vecs = jnp.stack([lru_conv_b[0], lru_ba[0], lru_bx[0], lru_lambda[0]])
    hy, h_p, hs = _scan(u, yb, cb_rows, h0_rows, lru_conv_w[0], vecs,
                        lru_wa[0].astype(bf16), lru_wx[0].astype(bf16), n_seq, t_pad, t_real)
    x = _proj_res(hy, x, lru_w_out[0].astype(bf16), norm_g[0, 3:4])
    x = ffn(x, 0, 1)
    k_p, v_p, k_s, v_s, kb, vb = _kv(x, kv_norm_g[None], attn_w_k.astype(bf16), attn_w_v.astype(bf16),
                                     n_seq, t_pad, t_real)

    lam_init = 0.8 - 0.6 * math.exp(-0.3 * 1)
    x = ffn(x, 1, 0)
    q = _q_proj(x, norm_g[1, 2:3], attn_w_q[0].astype(bf16))
    g_tiles, bias_s = _bias_tables(rel_bias, past_len, n_new)
    bias_s = bias_s.reshape(2 * N_HEADS * n_new, past_len + PAGE)
    lamv = jnp.stack([attn_lambda_q1[0], attn_lambda_k1[0], attn_lambda_q2[0], attn_lambda_k2[0]])
    sg = attn_subln_g[0][None]
    o = _attn_prompt(q, kb, vb, g_tiles, rel_bias, lamv, sg, n_seq, t_pad, lam_init)
    cache_kt = jnp.transpose(cache_k, (0, 2, 3, 4, 1)).reshape(n_pool, d, PAGE)
    cache_vr = cache_v.reshape(n_pool, PAGE * N_HEADS, d // N_HEADS)
    o = _attn_sample(q, k_s, v_s, cache_kt, cache_vr, page_table, bias_s, lamv, sg, o, rows_p, n_new, lam_init)
    x = _proj_res(o, x, attn_w_o[0].astype(bf16), norm_g[1, 3:4])
    x = ffn(x, 1, 1)

    def prompt_rows(a):
        return a[:rows_p].reshape(n_seq, t_pad, d)

    def sample_rows(a):
        return a[rows_p:].reshape(n_batch, n_new, d)

    y_prompt = prompt_rows(x)[:, N_META:t_real]
    y_sample = sample_rows(x)
    k_p = k_p.reshape(n_seq, t_real, N_HEADS, 2, HEAD_DIM)
    v_p = v_p.reshape(n_seq, t_real, N_HEADS, 2 * HEAD_DIM)
    conv_p = prompt_rows(u)[:, t_real - (CONV_W - 1):t_real][None]
    h_p = h_p.reshape(1, n_seq, d)
    k_s = k_s.reshape(n_batch, n_new, N_HEADS, 2, HEAD_DIM)
    v_s = v_s.reshape(n_batch, n_new, N_HEADS, 2 * HEAD_DIM)
    conv_s = sample_rows(u)[:, n_new - (CONV_W - 1):][None]
    h_s = hs.reshape(n_batch, n_new, d)[:, n_new - 1][None]
    return (y_prompt, y_sample, k_p, v_p, conv_p, h_p, k_s, v_s, conv_s, h_s)
```

```python
import functools
import math

import numpy as np
import jax
import jax.numpy as jnp
from jax import lax
from jax.experimental import pallas as pl
from jax.experimental.pallas import tpu as pltpu

f32 = jnp.float32
bf16 = jnp.bfloat16

N_META = 16
N_HEADS = 8
HEAD_DIM = 64
CONV_W = 4
LRU_BLOCKS = 4
LRU_C = 8.0
N_BUCKETS = 32
MAX_DIST = 128
NORM_EPS = 1e-6
SUBLN_EPS = 1e-5
NEG_INF = -1e30
PAGE = 128

V7X_VMEM_BYTES = 64 * 1024 * 1024
VMEM_LIMIT = V7X_VMEM_BYTES - 8 * 1024 * 1024
SUBLANES = 8
LANES = 128

ROW_TILE = 512
SEQ_TILE = 256
ATT_TILE = 512
ONES_ROWS = 16


def _cparams(n_axes, flags=None):
    return pltpu.CompilerParams(dimension_semantics=("arbitrary",) * n_axes,
                                vmem_limit_bytes=VMEM_LIMIT, flags=flags)


def _const_spec(shape):
    nd = len(shape)
    return pl.BlockSpec(shape, lambda *_: (0,) * nd, pipeline_mode=pl.Buffered(1))


def _rms(x, g, eps):
    ms = jnp.mean(x * x, axis=-1, keepdims=True)
    return x * lax.rsqrt(ms + eps) * g


def _dot(a, b):
    return jnp.dot(a, b, preferred_element_type=f32)


def _dot_nt(a, b):
    return lax.dot_general(a, b, (((1,), (1,)), ((), ())), preferred_element_type=f32)


def _ffn_kernel(x_ref, g_ref, wg_ref, wu_ref, wd_ref, o_ref):
    x = x_ref[...]
    xn = _rms(x, g_ref[0:1, :], NORM_EPS).astype(bf16)
    hg = _dot(xn, wg_ref[...])
    hu = _dot(xn, wu_ref[...])
    act = (jax.nn.silu(hg) * hu).astype(bf16)
    y = _dot(act, wd_ref[...])
    o_ref[...] = x + 0.5 * _rms(y, g_ref[1:2, :], NORM_EPS)


def _ffn(x, g2, wg, wu, wd):
    rows, d = x.shape
    ffn = wg.shape[1]
    row = pl.BlockSpec((ROW_TILE, d), lambda i: (i, 0))
    return pl.pallas_call(
        _ffn_kernel,
        grid=(rows // ROW_TILE,),
        in_specs=[row, _const_spec((2, d)), _const_spec((d, ffn)), _const_spec((d, ffn)),
                  _const_spec((ffn, d))],
        out_specs=row,
        out_shape=jax.ShapeDtypeStruct((rows, d), f32),
        compiler_params=_cparams(1),
        name="ffn",
    )(x, g2, wg, wu, wd)


def _lru_in_kernel(x_ref, g_ref, wy_ref, wx_ref, yb_ref, u_ref):
    h = _rms(x_ref[...], g_ref[...], NORM_EPS).astype(bf16)
    yb_ref[...] = jax.nn.gelu(_dot(h, wy_ref[...]))
    u_ref[...] = _dot(h, wx_ref[...])


def _lru_in(x, g, wy, wx):
    rows, d = x.shape
    row = pl.BlockSpec((ROW_TILE, d), lambda i: (i, 0))
    return pl.pallas_call(
        _lru_in_kernel,
        grid=(rows // ROW_TILE,),
        in_specs=[row, _const_spec((1, d)), _const_spec((d, d)), _const_spec((d, d))],
        out_specs=[row, row],
        out_shape=[jax.ShapeDtypeStruct((rows, d), f32)] * 2,
        compiler_params=_cparams(1),
        name="lru_in",
    )(x, g, wy, wx)


def _proj_res_kernel(a_ref, x_ref, w_ref, g_ref, o_ref):
    m = _dot(a_ref[...].astype(bf16), w_ref[...])
    o_ref[...] = x_ref[...] + _rms(m, g_ref[...], NORM_EPS)


def _proj_res(a, x, w, g):
    rows, d = x.shape
    row = pl.BlockSpec((ROW_TILE, d), lambda i: (i, 0))
    return pl.pallas_call(
        _proj_res_kernel,
        grid=(rows // ROW_TILE,),
        in_specs=[row, row, _const_spec((d, d)), _const_spec((1, d))],
        out_specs=row,
        out_shape=jax.ShapeDtypeStruct((rows, d), f32),
        compiler_params=_cparams(1),
        name="proj_res",
    )(a, x, w, g)


def _kv_kernel(n_prompt_tiles, x_ref, g_ref, wk_ref, wv_ref, kp_ref, vp_ref, ks_ref, vs_ref, kb_ref, vb_ref):
    s = pl.program_id(0)
    hk = _rms(x_ref[...], g_ref[...], NORM_EPS).astype(bf16)
    k = _dot(hk, wk_ref[...])
    v = _dot(hk, wv_ref[...])
    kb_ref[...] = k.astype(bf16)
    vb_ref[...] = v.astype(bf16)

    @pl.when(s < n_prompt_tiles)
    def _prompt():
        kp_ref[0] = k
        vp_ref[0] = v

    @pl.when(s >= n_prompt_tiles)
    def _sample():
        ks_ref[...] = k
        vs_ref[...] = v


def _kv(x, g, wk, wv, n_seq, t_pad, t_real):
    rows, d = x.shape
    n_seq_tiles = t_pad // SEQ_TILE
    n_prompt_tiles = n_seq * n_seq_tiles
    rows_s = rows - n_prompt_tiles * SEQ_TILE
    tile = pl.BlockSpec((SEQ_TILE, d), lambda s: (s, 0))
    prompt = pl.BlockSpec((1, SEQ_TILE, d),
                          lambda s: (jnp.minimum(s // n_seq_tiles, n_seq - 1),
                                     jnp.where(s < n_prompt_tiles, s % n_seq_tiles, n_seq_tiles - 1), 0))
    samp = pl.BlockSpec((SEQ_TILE, d), lambda s: (jnp.maximum(s - n_prompt_tiles, 0), 0))
    return pl.pallas_call(
        functools.partial(_kv_kernel, n_prompt_tiles),
        grid=(rows // SEQ_TILE,),
        in_specs=[tile, _const_spec((1, d)), _const_spec((d, d)), _const_spec((d, d))],
        out_specs=[prompt, prompt, samp, samp, tile, tile],
        out_shape=[jax.ShapeDtypeStruct((n_seq, t_real, d), f32)] * 2
        + [jax.ShapeDtypeStruct((rows_s, d), f32)] * 2
        + [jax.ShapeDtypeStruct((rows, d), bf16)] * 2,
        compiler_params=_cparams(1),
        name="kv_proj",
    )(x, g, wk, wv)


def _q_kernel(x_ref, g_ref, wq_ref, q_ref):
    h = _rms(x_ref[...], g_ref[...], NORM_EPS).astype(bf16)
    q_ref[...] = _dot(h, wq_ref[...]) * (HEAD_DIM ** -0.5)


def _q_proj(x, g, wq):
    rows, d = x.shape
    row = pl.BlockSpec((ROW_TILE, d), lambda i: (i, 0))
    return pl.pallas_call(
        _q_kernel,
        grid=(rows // ROW_TILE,),
        in_specs=[row, _const_spec((1, d)), _const_spec((d, d))],
        out_specs=row,
        out_shape=jax.ShapeDtypeStruct((rows, d), f32),
        compiler_params=_cparams(1),
        name="q_proj",
    )(x, g, wq)


def _log_sigmoid(x):
    return -(jnp.maximum(-x, 0.0) + jnp.log1p(jnp.exp(-jnp.abs(x))))


def _scan_groups(a, b):
    rows, c = a.shape
    a = a.reshape(rows // SUBLANES, SUBLANES, c)
    b = b.reshape(rows // SUBLANES, SUBLANES, c)
    t = lax.broadcasted_iota(jnp.int32, (1, SUBLANES, 1), 1)
    s = 1
    while s < SUBLANES:
        keep = t >= s
        a_s = jnp.where(keep, pltpu.roll(a, s, 1), 1.0)
        b_s = jnp.where(keep, pltpu.roll(b, s, 1), 0.0)
        b = a * b_s + b
        a = a * a_s
        s *= 2
    return a.reshape(rows, c), b.reshape(rows, c)


def _scan_kernel(n_seq_tiles, n_prompt_tiles, last_tile, last_row,
                 u_ref, yb_ref, cb_ref, h0_ref, cw_ref, vec_ref, wa_ref, wx_ref,
                 hy_ref, hp_ref, hs_ref, uext_ref, hcar_ref):
    s = pl.program_id(0)
    rows, c = u_ref.shape
    bw = c // LRU_BLOCKS
    u = u_ref[...]
    row = lax.broadcasted_iota(jnp.int32, (rows, 1), 0)
    conv_b, ba, bx, lam = (vec_ref[i:i + 1, :] for i in range(4))

    def gates_and_ab(uc):
        ucb = uc.astype(bf16)
        rp = jnp.concatenate([_dot(ucb[:, n * bw:(n + 1) * bw], wa_ref[n]) for n in range(LRU_BLOCKS)], axis=1)
        ip = jnp.concatenate([_dot(ucb[:, n * bw:(n + 1) * bw], wx_ref[n]) for n in range(LRU_BLOCKS)], axis=1)
        r = jax.nn.sigmoid(rp + ba)
        i = jax.nn.sigmoid(ip + bx)
        log_a = LRU_C * r * _log_sigmoid(lam)
        a = jnp.exp(log_a)
        b = jnp.sqrt(-jnp.tanh(log_a) * (a * a + 1.0)) * (i * uc)
        return a, b

    @pl.when(s < n_prompt_tiles)
    def _prompt():
        i = jnp.where(s >= n_seq_tiles, s - n_seq_tiles, s)

        @pl.when(i == 0)
        def _():
            uext_ref[0:SUBLANES, :] = jnp.zeros((SUBLANES, c), f32)
            hcar_ref[...] = jnp.zeros_like(hcar_ref)

        uext_ref[SUBLANES:SUBLANES + rows, :] = u
        uc = conv_b + cw_ref[CONV_W - 1:CONV_W, :] * u
        for j in range(1, CONV_W):
            uc = uc + cw_ref[CONV_W - 1 - j:CONV_W - j, :] * uext_ref[SUBLANES - j:SUBLANES - j + rows, :]
        a, b = gates_and_ab(uc)
        a, b = _scan_groups(a, b)
        carry = hcar_ref[...]
        groups = []
        for g in range(rows // SUBLANES):
            hg = a[g * SUBLANES:(g + 1) * SUBLANES] * carry + b[g * SUBLANES:(g + 1) * SUBLANES]
            carry = hg[SUBLANES - 1:SUBLANES]
            groups.append(hg)
        h = jnp.concatenate(groups, axis=0)
        hy_ref[...] = (h * yb_ref[...]).astype(hy_ref.dtype)
        uext_ref[0:SUBLANES, :] = u[rows - SUBLANES:rows]
        hcar_ref[...] = carry

        @pl.when(i == last_tile)
        def _():
            hp_ref[0] = h[last_row:last_row + 1]

    @pl.when(s >= n_prompt_tiles)
    def _sample():
        t = row % SUBLANES
        cb = cb_ref[...]
        uc = conv_b + cw_ref[CONV_W - 1:CONV_W, :] * u
        for j in range(1, CONV_W):
            uj = jnp.where(t >= j, pltpu.roll(u, j, 0), pltpu.roll(cb, rows - SUBLANES + j, 0))
            uc = uc + cw_ref[CONV_W - 1 - j:CONV_W - j, :] * uj
        a, b = gates_and_ab(uc)
        a, b = _scan_groups(a, b)
        h = a * h0_ref[...] + b
        hy_ref[...] = (h * yb_ref[...]).astype(hy_ref.dtype)
        hs_ref[...] = h


def _scan(u, yb, cb_rows, h0_rows, conv_w, vecs, wa, wx, n_seq, t_pad, t_real):
    rows, c = u.shape
    n_seq_tiles = t_pad // SEQ_TILE
    n_prompt_tiles = n_seq * n_seq_tiles
    n_tiles = rows // SEQ_TILE
    bw = c // LRU_BLOCKS
    tile = pl.BlockSpec((SEQ_TILE, c), lambda s: (s, 0))
    samp = pl.BlockSpec((SEQ_TILE, c), lambda s: (jnp.maximum(s - n_prompt_tiles, 0), 0))
    kern = functools.partial(_scan_kernel, n_seq_tiles, n_prompt_tiles,
                             (t_real - 1) // SEQ_TILE, (t_real - 1) % SEQ_TILE)
    return pl.pallas_call(
        kern,
        grid=(n_tiles,),
        in_specs=[tile, tile, samp, samp, _const_spec((CONV_W, c)), _const_spec((4, c)),
                  _const_spec((LRU_BLOCKS, bw, bw)), _const_spec((LRU_BLOCKS, bw, bw))],
        out_specs=[tile,
                   pl.BlockSpec((1, 1, c), lambda s: (jnp.minimum(s // n_seq_tiles, n_seq - 1), 0, 0)),
                   samp],
        out_shape=[jax.ShapeDtypeStruct((rows, c), bf16),
                   jax.ShapeDtypeStruct((n_seq, 1, c), f32),
                   jax.ShapeDtypeStruct((rows - n_prompt_tiles * SEQ_TILE, c), f32)],
        scratch_shapes=[pltpu.VMEM((SUBLANES + SEQ_TILE, c), f32), pltpu.VMEM((1, c), f32)],
        compiler_params=_cparams(1),
        name="lru_scan",
    )(u, yb, cb_rows, h0_rows, conv_w, vecs, wa, wx)


def _bucket_last_dist():
    n = np.arange(0, 4 * MAX_DIST, dtype=np.int32)
    max_exact = N_BUCKETS // 2
    nf = np.maximum(n, 1).astype(np.float32)
    large = max_exact + (np.log(nf / np.float32(max_exact)) / np.float32(math.log(MAX_DIST / max_exact))
                         * np.float32(N_BUCKETS - max_exact)).astype(np.int32)
    bucket = np.where(n < max_exact, n, np.minimum(large, N_BUCKETS - 1))
    assert np.all(np.diff(bucket) >= 0)
    last = [int(n[bucket == b].max()) if np.any(bucket == b) else None for b in range(N_BUCKETS)]
    assert last[N_BUCKETS - 1] == n[-1]
    return last


_BUCKET_LAST = _bucket_last_dist()
FAR_DIST = max(d for d in _BUCKET_LAST[:-1] if d is not None) + 1


def _bias_of_dist(dist, rb_ref, h):
    val = jnp.full(dist.shape, rb_ref[N_BUCKETS - 1, h], f32)
    for b in range(N_BUCKETS - 2, -1, -1):
        if _BUCKET_LAST[b] is not None:
            val = jnp.where(dist <= _BUCKET_LAST[b], rb_ref[b, h], val)
    return val


def _bias_kernel(past_len, rb_ref, g_ref, bs_ref):
    h = pl.program_id(0)
    _, nk, tq = g_ref.shape
    ki = lax.broadcasted_iota(jnp.int32, (nk, tq), 0)
    qi = lax.broadcasted_iota(jnp.int32, (nk, tq), 1)
    g_ref[0] = _bias_of_dist(qi - ki + (nk - tq), rb_ref, h)
    _, _, nq, wk = bs_ref.shape
    q = lax.broadcasted_iota(jnp.int32, (nq, wk), 0)
    col = lax.broadcasted_iota(jnp.int32, (nq, wk), 1)
    bias = _bias_of_dist(past_len + q - col, rb_ref, h)
    bs_ref[0, 0] = bias
    bs_ref[1, 0] = bias


def _bias_tables(rel_bias, past_len, n_new):
    g_shape = (N_HEADS, 2 * ATT_TILE, ATT_TILE)
    bs_shape = (2, N_HEADS, n_new, past_len + PAGE)
    return pl.pallas_call(
        functools.partial(_bias_kernel, past_len),
        grid=(N_HEADS,),
        in_specs=[pl.BlockSpec(memory_space=pltpu.SMEM)],
        out_specs=[pl.BlockSpec((1,) + g_shape[1:], lambda h: (h, 0, 0)),
                   pl.BlockSpec((2, 1) + bs_shape[2:], lambda h: (0, h, 0, 0))],
        out_shape=[jax.ShapeDtypeStruct(g_shape, f32), jax.ShapeDtypeStruct(bs_shape, f32)],
        compiler_params=_cparams(1),
        name="rel_bias_tables",
    )(rel_bias)


def _lambda(lamv_ref, lam_init):
    l1 = jnp.sum(lamv_ref[0:1, :] * lamv_ref[1:2, :], axis=-1, keepdims=True)
    l2 = jnp.sum(lamv_ref[2:3, :] * lamv_ref[3:4, :], axis=-1, keepdims=True)
    return jnp.exp(l1) - jnp.exp(l2) + lam_init


def _loop(n, body, init):
    if isinstance(n, int) and n == 0:
        return init
    return lax.fori_loop(0, n, body, init)


def _rows(start, n):
    if isinstance(start, int):
        return pl.ds(start, n)
    return pl.ds(pl.multiple_of(start, SEQ_TILE), n)


def _attn_prompt_kernel(lam_init, rb_ref, q_ref, k_ref, v_ref, g_ref, lamv_ref, sgc_ref, o_ref, vt_ref, s_ref):
    h = pl.program_id(1)
    t_pad, hw = q_ref.shape
    tk, tc = ATT_TILE, SEQ_TILE
    n_full, tail = divmod(t_pad, tk)
    far_bias = rb_ref[N_BUCKETS - 1, h]
    lam = _lambda(lamv_ref, lam_init)
    sgc = sgc_ref[...]

    ones_rows = (lax.broadcasted_iota(jnp.int32, (ONES_ROWS, tk), 0) == 0).astype(bf16)

    def fill_vt(j, n):
        vt_ref[j, 0:hw, 0:n] = v_ref[_rows(j * tk, n), :].astype(f32).T.astype(bf16)
        vt_ref[j, hw:hw + ONES_ROWS, :] = ones_rows

    def fill_body(j, carry):
        fill_vt(j, tk)
        return carry

    lax.fori_loop(0, n_full, fill_body, 0)
    if tail:
        fill_vt(n_full, tail)

    row = lax.broadcasted_iota(jnp.int32, (hw, tc), 0)

    def chain_q(q0, c):
        qt = q_ref[_rows(q0 + c * tc, tc), :].T
        return jnp.concatenate([jnp.where(row < HEAD_DIM, qt, 0.0), jnp.where(row >= HEAD_DIM, qt, 0.0)],
                               axis=1).astype(bf16)

    def update(state, s, off, vt):
        m, acc = state
        smax = jnp.max(s, axis=0, keepdims=True)
        m_new = jnp.maximum(m, smax if off is None else smax + off)
        alpha = jnp.exp(m - m_new)
        p = jnp.exp(s - (m_new if off is None else m_new - off))
        acc = alpha * acc + _dot(vt, p.astype(bf16))
        return m_new, acc

    def attend(q0, n_chain, n_prev, has_near):
        qq = [chain_q(q0, c) for c in range(n_chain)]
        states = tuple((jnp.full((1, 2 * tc), NEG_INF, f32), jnp.zeros((hw + ONES_ROWS, 2 * tc), f32))
                       for _ in range(n_chain))

        def logits(j, slot):
            kb = k_ref[_rows(j * tk, tk), :]
            for c in range(n_chain):
                s_ref[slot, c] = _dot(kb, qq[c])

        def far_step(j, slot, states):
            logits(j + 1, 1 - slot)
            vt = vt_ref[j]
            return tuple(update(states[c], s_ref[slot, c], far_bias, vt) for c in range(n_chain))

        def far_pair(j, st):
            return far_step(j + 1, 1, far_step(j, 0, st))

        if has_near:
            n_far = n_prev - 1
            n_quad = n_far >> 2
            pair = (n_far >> 1) & 1
            odd = n_far & 1
            logits(0, 0)
            states = _loop(n_quad, lambda i, st: far_pair(4 * i + 2, far_pair(4 * i, st)), states)
            states = _loop(pair, lambda i, st: far_pair(4 * n_quad, st), states)
            states = _loop(odd, lambda i, st: far_step(n_far - 1, 0, st), states)

        for c in range(n_chain):
            nk = (c + 1) * tc
            cols = slice(c * tc, (c + 1) * tc)
            g = g_ref[0, tk:tk + nk, cols]
            ki = lax.broadcasted_iota(jnp.int32, (nk, 2 * tc), 0)
            qi = c * tc + (lax.broadcasted_iota(jnp.int32, (nk, 2 * tc), 1) & (tc - 1))
            s = _dot(k_ref[_rows(q0, nk), :], qq[c]) + jnp.concatenate([g, g], axis=1)
            s = jnp.where(ki <= qi, s, NEG_INF)
            vt = vt_ref[n_prev][:, 0:nk]
            if has_near:
                g = g_ref[0, 0:tk, cols]
                s = jnp.concatenate([s_ref[odd, c] + jnp.concatenate([g, g], axis=1), s], axis=0)
                vt = jnp.concatenate([vt_ref[n_far], vt], axis=1)
            _, acc = update(states[c], s, None, vt)
            on = acc[0:hw] * (1.0 / acc[hw:hw + 1])
            ot = on[:, 0:tc] - lam * on[:, tc:2 * tc]
            ms = jnp.mean(ot * ot, axis=0, keepdims=True)
            ot = ot * lax.rsqrt(ms + SUBLN_EPS) * sgc * (1.0 - lam_init)
            o_ref[_rows(q0 + c * tc, tc), :] = ot.T

    def tile_body(p, carry):
        attend(p * tk, tk // tc, p, True)
        return carry

    attend(0, tk // tc, 0, False)
    lax.fori_loop(1, n_full, tile_body, 0)
    if tail:
        attend(n_full * tk, tail // tc, n_full, True)


def _attn_prompt(q, kb, vb, g_tiles, rel_bias, lamv, subln_g, n_seq, t_pad, lam_init):
    rows, d = q.shape
    hw = d // N_HEADS
    assert ATT_TILE >= FAR_DIST and ATT_TILE % SEQ_TILE == 0 and t_pad % SEQ_TILE == 0
    blk = lambda b, h: (b, h)
    sgc = subln_g.reshape(hw, 1)
    return pl.pallas_call(
        functools.partial(_attn_prompt_kernel, lam_init),
        grid=(n_seq, N_HEADS),
        in_specs=[pl.BlockSpec(memory_space=pltpu.SMEM),
                  pl.BlockSpec((t_pad, hw), blk), pl.BlockSpec((t_pad, hw), blk),
                  pl.BlockSpec((t_pad, hw), blk),
                  pl.BlockSpec((1,) + g_tiles.shape[1:], lambda b, h: (h, 0, 0)),
                  _const_spec(lamv.shape), _const_spec(sgc.shape)],
        out_specs=pl.BlockSpec((t_pad, hw), blk),
        out_shape=jax.ShapeDtypeStruct((rows, d), f32),
        scratch_shapes=[pltpu.VMEM((pl.cdiv(t_pad, ATT_TILE), hw + ONES_ROWS, ATT_TILE), bf16),
                        pltpu.VMEM((2, ATT_TILE // SEQ_TILE, ATT_TILE, 2 * SEQ_TILE), f32)],
        compiler_params=_cparams(2),
        name="attn_prompt",
    )(rel_bias, q, kb, vb, g_tiles, lamv, sgc)


def _attn_sample_kernel(lam_init, n_pages, pt_ref, q_ref, kn_ref, vn_ref, bias_ref, lamv_ref, sg_ref,
                        *refs):
    k_refs = refs[:n_pages]
    v_refs = refs[n_pages:2 * n_pages]
    o_ref = refs[2 * n_pages + 1]
    n_new, d = q_ref.shape
    n_grp = 2 * N_HEADS
    n_rows = n_grp * n_new
    hw = d // N_HEADS
    lam = _lambda(lamv_ref, lam_init)

    qt = jnp.concatenate([q_ref[...]] * n_grp, axis=0)
    r = lax.broadcasted_iota(jnp.int32, (n_rows, d), 0)
    col = lax.broadcasted_iota(jnp.int32, (n_rows, d), 1)
    row_grp = ((r // n_new) % N_HEADS) * 2 + r // (N_HEADS * n_new)
    wq = jnp.where(col // HEAD_DIM == row_grp, qt, 0.0).astype(bf16)

    pad = jnp.zeros((PAGE - n_new, d), f32)
    k_new = jnp.concatenate([kn_ref[...], pad], axis=0).astype(bf16)
    v_new = jnp.concatenate([vn_ref[...], pad], axis=0).astype(bf16)

    s = jnp.concatenate([_dot(wq, k_refs[j][0].astype(bf16)) for j in range(n_pages)]
                        + [_dot_nt(wq, k_new)], axis=1)
    s = s + bias_ref[...]
    past = n_pages * PAGE
    rr = lax.broadcasted_iota(jnp.int32, s.shape, 0)
    cc = lax.broadcasted_iota(jnp.int32, s.shape, 1)
    s = jnp.where(cc - past <= rr % n_new, s, NEG_INF)
    m = jnp.max(s, axis=-1, keepdims=True)
    p = jnp.exp(s - m)
    p = p / jnp.sum(p, axis=-1, keepdims=True)
    half = N_HEADS * n_new
    w = (p[0:half] - lam * p[half:2 * half]).astype(bf16)
    sg = sg_ref[...]
    heads = []
    for hh in range(N_HEADS):
        vh = jnp.concatenate([v_refs[j][0, pl.ds(hh, PAGE, stride=N_HEADS), :].astype(bf16)
                              for j in range(n_pages)] + [v_new[:, hh * hw:(hh + 1) * hw]], axis=0)
        out = _dot(w[hh * n_new:(hh + 1) * n_new, :], vh)
        heads.append(_rms(out, sg, SUBLN_EPS))
    o_ref[...] = jnp.concatenate(heads, axis=1) * (1.0 - lam_init)


def _attn_sample(q, k_new, v_new, cache_k, cache_v, page_table, bias_s, lamv, subln_g, o_all,
                 row0, n_new, lam_init):
    n_batch, n_pages = page_table.shape
    d = q.shape[1]
    blk0 = row0 // n_new
    tok = pl.BlockSpec((n_new, d), lambda b, pt: (blk0 + b, 0))
    new = pl.BlockSpec((n_new, d), lambda b, pt: (b, 0))

    def page_spec(j):
        return pl.BlockSpec((1,) + cache_k.shape[1:], lambda b, pt: (pt[b, j], 0, 0))

    grid_spec = pltpu.PrefetchScalarGridSpec(
        num_scalar_prefetch=1,
        grid=(n_batch,),
        in_specs=[tok, new, new,
                  pl.BlockSpec(bias_s.shape, lambda b, pt: (0, 0), pipeline_mode=pl.Buffered(1)),
                  pl.BlockSpec(lamv.shape, lambda b, pt: (0, 0), pipeline_mode=pl.Buffered(1)),
                  pl.BlockSpec(subln_g.shape, lambda b, pt: (0, 0), pipeline_mode=pl.Buffered(1))]
        + [page_spec(j) for j in range(n_pages)] * 2
        + [pl.BlockSpec(memory_space=pl.ANY)],
        out_specs=tok,
    )
    n_in = 1 + 6 + 2 * n_pages
    return pl.pallas_call(
        functools.partial(_attn_sample_kernel, lam_init, n_pages),
        grid_spec=grid_spec,
        out_shape=jax.ShapeDtypeStruct(o_all.shape, o_all.dtype),
        input_output_aliases={n_in: 0},
        compiler_params=_cparams(1),
        name="attn_sample",
    )(page_table, q, k_new, v_new, bias_s, lamv, subln_g,
      *([cache_k] * n_pages), *([cache_v] * n_pages), o_all)


def kernel(x_prompt, x_sample, cache_k, cache_v, page_table, state_conv, state_h, meta_tokens, norm_g, kv_norm_g, ffn_w_gate, ffn_w_up, ffn_w_down, lru_w_y, lru_w_x, lru_conv_w, lru_conv_b, lru_wa, lru_ba, lru_wx, lru_bx, lru_lambda, lru_w_out, attn_w_q, attn_w_k, attn_w_v, attn_w_o, attn_lambda_q1, attn_lambda_k1, attn_lambda_q2, attn_lambda_k2, attn_subln_g, rel_bias):
    n_seq, seq, d = x_prompt.shape
    n_batch, n_new, _ = x_sample.shape
    n_pool, page, _, _, _ = cache_k.shape
    n_pages = page_table.shape[1]
    depth = norm_g.shape[0]
    assert page == PAGE and n_new == SUBLANES and depth == 2 and lru_w_y.shape[0] == 1
    t_real = seq + N_META
    t_pad = -(-t_real // SEQ_TILE) * SEQ_TILE
    rows_p = n_seq * t_pad
    rows = rows_p + n_batch * n_new
    assert rows % ROW_TILE == 0 and rows_p % SEQ_TILE == 0 and (rows - rows_p) % SEQ_TILE == 0
    past_len = n_pages * PAGE

    meta = meta_tokens.astype(x_prompt.dtype)
    zpad = jnp.zeros((t_pad - t_real, d), x_prompt.dtype)
    parts = []
    for b in range(n_seq):
        parts += [meta, x_prompt[b], zpad]
    x = jnp.concatenate(parts + [x_sample.reshape(n_batch * n_new, d)], axis=0)

    def ffn(x, l, i):
        return _ffn(x, norm_g[l, 2 * i * 2:2 * i * 2 + 2],
                    ffn_w_gate[l, i].astype(bf16), ffn_w_up[l, i].astype(bf16),
                    ffn_w_down[l, i].astype(bf16))

    x = ffn(x, 0, 0)
    yb, u = _lru_in(x, norm_g[0, 2:3], lru_w_y[0].astype(bf16), lru_w_x[0].astype(bf16))
    cb_rows = jnp.pad(state_conv[0], ((0, 0), (SUBLANES - (CONV_W - 1), 0), (0, 0))).reshape(n_batch * n_new, d)
    h0_rows = jnp.repeat(state_h[0], n_new, axis=0)
    vecs = jnp.stack([lru_conv_b[0], lru_ba[0], lru_bx[0], lru_lambda[0]])
    hy, h_p, hs = _scan(u, yb, cb_rows, h0_rows, lru_conv_w[0], vecs,
                        lru_wa[0].astype(bf16), lru_wx[0].astype(bf16), n_seq, t_pad, t_real)
    x = _proj_res(hy, x, lru_w_out[0].astype(bf16), norm_g[0, 3:4])
    x = ffn(x, 0, 1)
    k_p, v_p, k_s, v_s, kb, vb = _kv(x, kv_norm_g[None], attn_w_k.astype(bf16), attn_w_v.astype(bf16),
                                     n_seq, t_pad, t_real)

    lam_init = 0.8 - 0.6 * math.exp(-0.3 * 1)
    x = ffn(x, 1, 0)
    q = _q_proj(x, norm_g[1, 2:3], attn_w_q[0].astype(bf16))
    g_tiles, bias_s = _bias_tables(rel_bias, past_len, n_new)
    bias_s = bias_s.reshape(2 * N_HEADS * n_new, past_len + PAGE)
    lamv = jnp.stack([attn_lambda_q1[0], attn_lambda_k1[0], attn_lambda_q2[0], attn_lambda_k2[0]])
    sg = attn_subln_g[0][None]
    o = _attn_prompt(q, kb, vb, g_tiles, rel_bias, lamv, sg, n_seq, t_pad, lam_init)
    cache_kt = jnp.transpose(cache_k, (0, 2, 3, 4, 1)).reshape(n_pool, d, PAGE)
    cache_vr = cache_v.reshape(n_pool, PAGE * N_HEADS, d // N_HEADS)
    o = _attn_sample(q, k_s, v_s, cache_kt, cache_vr, page_table, bias_s, lamv, sg, o, rows_p, n_new, lam_init)
    x = _proj_res(o, x, attn_w_o[0].astype(bf16), norm_g[1, 3:4])
    x = ffn(x, 1, 1)

    def prompt_rows(a, lo, hi):
        return jnp.stack([a[b * t_pad + lo:b * t_pad + hi] for b in range(n_seq)])

    def sample_rows(a):
        return a[rows_p:].reshape(n_batch, n_new, d)

    y_prompt = prompt_rows(x, N_META, t_real)
    y_sample = sample_rows(x)
    k_p = k_p.reshape(n_seq, t_real, N_HEADS, 2, HEAD_DIM)
    v_p = v_p.reshape(n_seq, t_real, N_HEADS, 2 * HEAD_DIM)
    conv_p = prompt_rows(u, t_real - (CONV_W - 1), t_real)[None]
    h_p = h_p.reshape(1, n_seq, d)
    k_s = k_s.reshape(n_batch, n_new, N_HEADS, 2, HEAD_DIM)
    v_s = v_s.reshape(n_batch, n_new, N_HEADS, 2 * HEAD_DIM)
    conv_s = sample_rows(u)[:, n_new - (CONV_W - 1):][None]
    h_s = hs.reshape(n_batch, n_new, d)[:, n_new - 1][None]
    return (y_prompt, y_sample, k_p, v_p, conv_p, h_p, k_s, v_s, conv_s, h_s)
```

```python
import functools
import math

import numpy as np
import jax
import jax.numpy as jnp
from jax import lax
from jax.experimental import pallas as pl
from jax.experimental.pallas import tpu as pltpu

f32 = jnp.float32
bf16 = jnp.bfloat16

N_META = 16
N_HEADS = 8
HEAD_DIM = 64
CONV_W = 4
LRU_BLOCKS = 4
LRU_C = 8.0
N_BUCKETS = 32
MAX_DIST = 128
NORM_EPS = 1e-6
SUBLN_EPS = 1e-5
NEG_INF = -1e30
PAGE = 128

V7X_VMEM_BYTES = 64 * 1024 * 1024
VMEM_LIMIT = V7X_VMEM_BYTES - 8 * 1024 * 1024
SUBLANES = 8
LANES = 128

ROW_TILE = 512
SEQ_TILE = 256
ATT_TILE = 512
ONES_ROWS = 16


def _cparams(n_axes, flags=None):
    return pltpu.CompilerParams(dimension_semantics=("arbitrary",) * n_axes,
                                vmem_limit_bytes=VMEM_LIMIT, flags=flags)


def _const_spec(shape):
    nd = len(shape)
    return pl.BlockSpec(shape, lambda *_: (0,) * nd, pipeline_mode=pl.Buffered(1))


def _rms(x, g, eps):
    ms = jnp.mean(x * x, axis=-1, keepdims=True)
    return x * lax.rsqrt(ms + eps) * g


def _dot(a, b):
    return jnp.dot(a, b, preferred_element_type=f32)


def _dot_nt(a, b):
    return lax.dot_general(a, b, (((1,), (1,)), ((), ())), preferred_element_type=f32)


def _ffn_kernel(x_ref, g_ref, wg_ref, wu_ref, wd_ref, o_ref):
    x = x_ref[...]
    xn = _rms(x, g_ref[0:1, :], NORM_EPS).astype(bf16)
    hg = _dot(xn, wg_ref[...])
    hu = _dot(xn, wu_ref[...])
    act = (jax.nn.silu(hg) * hu).astype(bf16)
    y = _dot(act, wd_ref[...])
    o_ref[...] = x + 0.5 * _rms(y, g_ref[1:2, :], NORM_EPS)


def _ffn(x, g2, wg, wu, wd, l, i):
    rows, d = x.shape
    ffn = wg.shape[-1]

    def weight(shape):
        return pl.BlockSpec((None, None) + shape, lambda _: (l, i, 0, 0), pipeline_mode=pl.Buffered(1))

    row = pl.BlockSpec((ROW_TILE, d), lambda i: (i, 0))
    return pl.pallas_call(
        _ffn_kernel,
        grid=(rows // ROW_TILE,),
        in_specs=[row, _const_spec((2, d)), weight((d, ffn)), weight((d, ffn)), weight((ffn, d))],
        out_specs=row,
        out_shape=jax.ShapeDtypeStruct((rows, d), f32),
        compiler_params=_cparams(1),
        name="ffn",
    )(x, g2, wg, wu, wd)


def _proj_res_kernel(first_tiles, *refs):
    a_refs = refs[:len(first_tiles)]
    x_ref, w_ref, g_ref, o_ref = refs[len(first_tiles):]
    i = pl.program_id(0)
    a = a_refs[0][...]
    for first, a_ref in zip(first_tiles[1:], a_refs[1:]):
        a = jnp.where(i >= first, a_ref[...], a)
    m = _dot(a.astype(bf16), w_ref[...])
    o_ref[...] = x_ref[...] + _rms(m, g_ref[...], NORM_EPS)


def _proj_res(a_parts, x, w, g):
    rows, d = x.shape
    row = pl.BlockSpec((ROW_TILE, d), lambda i: (i, 0))
    first_tiles, a_specs, first = [], [], 0
    for a in a_parts:
        assert a.shape[0] % ROW_TILE == 0
        n = a.shape[0] // ROW_TILE
        a_specs.append(pl.BlockSpec((ROW_TILE, d), functools.partial(
            lambda i, first, n: (jnp.clip(i - first, 0, n - 1), 0), first=first, n=n)))
        first_tiles.append(first)
        first += n
    assert first == rows // ROW_TILE
    return pl.pallas_call(
        functools.partial(_proj_res_kernel, tuple(first_tiles)),
        grid=(rows // ROW_TILE,),
        in_specs=a_specs + [row, _const_spec((d, d)), _const_spec((1, d))],
        out_specs=row,
        out_shape=jax.ShapeDtypeStruct((rows, d), f32),
        compiler_params=_cparams(1),
        name="proj_res",
    )(*a_parts, x, w, g)


def _kv_kernel(n_prompt_tiles, x_ref, g_ref, wk_ref, wv_ref, kp_ref, vp_ref, ks_ref, vs_ref, kb_ref, vb_ref):
    s = pl.program_id(0)
    hk = _rms(x_ref[...], g_ref[...], NORM_EPS).astype(bf16)
    k = _dot(hk, wk_ref[...])
    v = _dot(hk, wv_ref[...])
    kb_ref[...] = k.astype(bf16)
    vb_ref[...] = v.astype(bf16)

    @pl.when(s < n_prompt_tiles)
    def _prompt():
        kp_ref[0] = k
        vp_ref[0] = v

    @pl.when(s >= n_prompt_tiles)
    def _sample():
        ks_ref[...] = k
        vs_ref[...] = v


def _kv(x, g, wk, wv, n_seq, t_pad, t_real):
    rows, d = x.shape
    n_seq_tiles = t_pad // SEQ_TILE
    n_prompt_tiles = n_seq * n_seq_tiles
    rows_s = rows - n_prompt_tiles * SEQ_TILE
    tile = pl.BlockSpec((SEQ_TILE, d), lambda s: (s, 0))
    prompt = pl.BlockSpec((1, SEQ_TILE, d),
                          lambda s: (jnp.minimum(s // n_seq_tiles, n_seq - 1),
                                     jnp.where(s < n_prompt_tiles, s % n_seq_tiles, n_seq_tiles - 1), 0))
    samp = pl.BlockSpec((SEQ_TILE, d), lambda s: (jnp.maximum(s - n_prompt_tiles, 0), 0))
    return pl.pallas_call(
        functools.partial(_kv_kernel, n_prompt_tiles),
        grid=(rows // SEQ_TILE,),
        in_specs=[tile, _const_spec((1, d)), _const_spec((d, d)), _const_spec((d, d))],
        out_specs=[prompt, prompt, samp, samp, tile, tile],
        out_shape=[jax.ShapeDtypeStruct((n_seq, t_real, d), f32)] * 2
        + [jax.ShapeDtypeStruct((rows_s, d), f32)] * 2
        + [jax.ShapeDtypeStruct((rows, d), bf16)] * 2,
        compiler_params=_cparams(1),
        name="kv_proj",
    )(x, g, wk, wv)


def _q_kernel(x_ref, g_ref, wq_ref, q_ref):
    h = _rms(x_ref[...], g_ref[...], NORM_EPS).astype(bf16)
    q_ref[...] = _dot(h, wq_ref[...]) * (HEAD_DIM ** -0.5)


def _q_proj(x, g, wq):
    rows, d = x.shape
    row = pl.BlockSpec((ROW_TILE, d), lambda i: (i, 0))
    return pl.pallas_call(
        _q_kernel,
        grid=(rows // ROW_TILE,),
        in_specs=[row, _const_spec((1, d)), _const_spec((d, d))],
        out_specs=row,
        out_shape=jax.ShapeDtypeStruct((rows, d), f32),
        compiler_params=_cparams(1),
        name="q_proj",
    )(x, g, wq)


def _log_sigmoid(x):
    return -(jnp.maximum(-x, 0.0) + jnp.log1p(jnp.exp(-jnp.abs(x))))


def _scan_groups(a, b):
    rows, c = a.shape
    a = a.reshape(rows // SUBLANES, SUBLANES, c)
    b = b.reshape(rows // SUBLANES, SUBLANES, c)
    t = lax.broadcasted_iota(jnp.int32, (1, SUBLANES, 1), 1)
    s = 1
    while s < SUBLANES:
        keep = t >= s
        a_s = jnp.where(keep, pltpu.roll(a, s, 1), 1.0)
        b_s = jnp.where(keep, pltpu.roll(b, s, 1), 0.0)
        b = a * b_s + b
        a = a * a_s
        s *= 2
    return a.reshape(rows, c), b.reshape(rows, c)


def _lru_kernel(n_seq_tiles, n_prompt_tiles, last_tile, last_row,
                x_ref, cb_ref, h0_ref, g_ref, wy_ref, wx_ref, cw_ref, vec_ref, wa_ref, wi_ref, wo_ref,
                xo_ref, hp_ref, cp_ref, hs_ref, us_ref, uext_ref, hcar_ref):
    s = pl.program_id(0)
    rows, c = x_ref.shape
    bw = c // LRU_BLOCKS
    conv_b, ba, bx, lam = (vec_ref[i:i + 1, :] for i in range(4))

    def project_in():
        hn = _rms(x_ref[...], g_ref[0:1, :], NORM_EPS).astype(bf16)
        return _dot(hn, wx_ref[...]), hn

    def project_out(h, hn):
        hy = (h * jax.nn.gelu(_dot(hn, wy_ref[...]))).astype(bf16)
        xo_ref[...] = x_ref[...] + _rms(_dot(hy, wo_ref[...]), g_ref[1:2, :], NORM_EPS)

    def gates_and_ab(uc):
        ucb = uc.astype(bf16)
        rp = jnp.concatenate([_dot(ucb[:, n * bw:(n + 1) * bw], wa_ref[n]) for n in range(LRU_BLOCKS)], axis=1)
        ip = jnp.concatenate([_dot(ucb[:, n * bw:(n + 1) * bw], wi_ref[n]) for n in range(LRU_BLOCKS)], axis=1)
        r = jax.nn.sigmoid(rp + ba)
        i = jax.nn.sigmoid(ip + bx)
        log_a = LRU_C * r * _log_sigmoid(lam)
        a = jnp.exp(log_a)
        b = jnp.sqrt(-jnp.tanh(log_a) * (a * a + 1.0)) * (i * uc)
        return a, b

    @pl.when(s < n_prompt_tiles)
    def _prompt():
        i = jnp.where(s >= n_seq_tiles, s - n_seq_tiles, s)

        @pl.when(i == 0)
        def _():
            uext_ref[0:SUBLANES, :] = jnp.zeros((SUBLANES, c), f32)
            hcar_ref[...] = jnp.zeros_like(hcar_ref)

        u, hn = project_in()
        uext_ref[SUBLANES:SUBLANES + rows, :] = u
        uc = conv_b + cw_ref[CONV_W - 1:CONV_W, :] * u
        for j in range(1, CONV_W):
            uc = uc + cw_ref[CONV_W - 1 - j:CONV_W - j, :] * uext_ref[SUBLANES - j:SUBLANES - j + rows, :]
        a, b = gates_and_ab(uc)
        a, b = _scan_groups(a, b)
        carry = hcar_ref[...]
        groups = []
        for g in range(rows // SUBLANES):
            hg = a[g * SUBLANES:(g + 1) * SUBLANES] * carry + b[g * SUBLANES:(g + 1) * SUBLANES]
            carry = hg[SUBLANES - 1:SUBLANES]
            groups.append(hg)
        h = jnp.concatenate(groups, axis=0)
        project_out(h, hn)
        uext_ref[0:SUBLANES, :] = u[rows - SUBLANES:rows]
        hcar_ref[...] = carry

        @pl.when(i == last_tile)
        def _():
            grp = last_row // SUBLANES * SUBLANES
            hp_ref[0] = h[last_row:last_row + 1]
            cp_ref[0] = u[grp:grp + SUBLANES]

    @pl.when(s >= n_prompt_tiles)
    def _sample():
        u, hn = project_in()
        t = lax.broadcasted_iota(jnp.int32, (rows, 1), 0) % SUBLANES
        cb = cb_ref[...]
        uc = conv_b + cw_ref[CONV_W - 1:CONV_W, :] * u
        for j in range(1, CONV_W):
            uj = jnp.where(t >= j, pltpu.roll(u, j, 0), pltpu.roll(cb, rows - SUBLANES + j, 0))
            uc = uc + cw_ref[CONV_W - 1 - j:CONV_W - j, :] * uj
        a, b = gates_and_ab(uc)
        a, b = _scan_groups(a, b)
        h = a * h0_ref[...] + b
        project_out(h, hn)
        hs_ref[...] = h
        us_ref[...] = u


def _lru_layer(x, cb_rows, h0_rows, g2, wy, wx, conv_w, vecs, wa, wi, wo, n_seq, t_pad, t_real):
    rows, c = x.shape
    n_seq_tiles = t_pad // SEQ_TILE
    n_prompt_tiles = n_seq * n_seq_tiles
    n_tiles = rows // SEQ_TILE
    rows_s = rows - n_prompt_tiles * SEQ_TILE
    bw = c // LRU_BLOCKS
    last_row = (t_real - 1) % SEQ_TILE
    assert last_row % SUBLANES >= CONV_W - 2
    tile = pl.BlockSpec((SEQ_TILE, c), lambda s: (s, 0))
    samp = pl.BlockSpec((SEQ_TILE, c), lambda s: (jnp.maximum(s - n_prompt_tiles, 0), 0))

    def per_seq(n):
        return pl.BlockSpec((1, n, c), lambda s: (jnp.minimum(s // n_seq_tiles, n_seq - 1), 0, 0))

    kern = functools.partial(_lru_kernel, n_seq_tiles, n_prompt_tiles, (t_real - 1) // SEQ_TILE, last_row)
    return pl.pallas_call(
        kern,
        grid=(n_tiles,),
        in_specs=[tile, samp, samp, _const_spec((2, c)), _const_spec((c, c)), _const_spec((c, c)),
                  _const_spec((CONV_W, c)), _const_spec((4, c)),
                  _const_spec((LRU_BLOCKS, bw, bw)), _const_spec((LRU_BLOCKS, bw, bw)), _const_spec((c, c))],
        out_specs=[tile, per_seq(1), per_seq(SUBLANES), samp, samp],
        out_shape=[jax.ShapeDtypeStruct((rows, c), f32),
                   jax.ShapeDtypeStruct((n_seq, 1, c), f32),
                   jax.ShapeDtypeStruct((n_seq, SUBLANES, c), f32),
                   jax.ShapeDtypeStruct((rows_s, c), f32),
                   jax.ShapeDtypeStruct((rows_s, c), f32)],
        scratch_shapes=[pltpu.VMEM((SUBLANES + SEQ_TILE, c), f32), pltpu.VMEM((1, c), f32)],
        compiler_params=_cparams(1),
        name="lru_layer",
    )(x, cb_rows, h0_rows, g2, wy, wx, conv_w, vecs, wa, wi, wo)


def _bucket_last_dist():
    n = np.arange(0, 4 * MAX_DIST, dtype=np.int32)
    max_exact = N_BUCKETS // 2
    nf = np.maximum(n, 1).astype(np.float32)
    large = max_exact + (np.log(nf / np.float32(max_exact)) / np.float32(math.log(MAX_DIST / max_exact))
                         * np.float32(N_BUCKETS - max_exact)).astype(np.int32)
    bucket = np.where(n < max_exact, n, np.minimum(large, N_BUCKETS - 1))
    assert np.all(np.diff(bucket) >= 0)
    last = [int(n[bucket == b].max()) if np.any(bucket == b) else None for b in range(N_BUCKETS)]
    assert last[N_BUCKETS - 1] == n[-1]
    return last


_BUCKET_LAST = _bucket_last_dist()
FAR_DIST = max(d for d in _BUCKET_LAST[:-1] if d is not None) + 1


def _bias_of_dist(dist, rb_ref, h):
    val = jnp.full(dist.shape, rb_ref[N_BUCKETS - 1, h], f32)
    for b in range(N_BUCKETS - 2, -1, -1):
        if _BUCKET_LAST[b] is not None:
            val = jnp.where(dist <= _BUCKET_LAST[b], rb_ref[b, h], val)
    return val


def _bias_kernel(past_len, rb_ref, g_ref, bs_ref):
    h = pl.program_id(0)
    _, nk, tq = g_ref.shape
    ki = lax.broadcasted_iota(jnp.int32, (nk, tq), 0)
    qi = lax.broadcasted_iota(jnp.int32, (nk, tq), 1)
    g_ref[0] = _bias_of_dist(qi - ki + (nk - tq), rb_ref, h)
    _, _, nq, wk = bs_ref.shape
    q = lax.broadcasted_iota(jnp.int32, (nq, wk), 0)
    col = lax.broadcasted_iota(jnp.int32, (nq, wk), 1)
    bias = _bias_of_dist(past_len + q - col, rb_ref, h)
    bs_ref[0, 0] = bias
    bs_ref[1, 0] = bias


def _bias_tables(rel_bias, past_len, n_new):
    g_shape = (N_HEADS, 2 * ATT_TILE, ATT_TILE)
    bs_shape = (2, N_HEADS, n_new, past_len + PAGE)
    return pl.pallas_call(
        functools.partial(_bias_kernel, past_len),
        grid=(N_HEADS,),
        in_specs=[pl.BlockSpec(memory_space=pltpu.SMEM)],
        out_specs=[pl.BlockSpec((1,) + g_shape[1:], lambda h: (h, 0, 0)),
                   pl.BlockSpec((2, 1) + bs_shape[2:], lambda h: (0, h, 0, 0))],
        out_shape=[jax.ShapeDtypeStruct(g_shape, f32), jax.ShapeDtypeStruct(bs_shape, f32)],
        compiler_params=_cparams(1),
        name="rel_bias_tables",
    )(rel_bias)


def _lambda(lamv_ref, lam_init):
    l1 = jnp.sum(lamv_ref[0:1, :] * lamv_ref[1:2, :], axis=-1, keepdims=True)
    l2 = jnp.sum(lamv_ref[2:3, :] * lamv_ref[3:4, :], axis=-1, keepdims=True)
    return jnp.exp(l1) - jnp.exp(l2) + lam_init


def _loop(n, body, init):
    if isinstance(n, int) and n == 0:
        return init
    return lax.fori_loop(0, n, body, init)


def _rows(start, n):
    if isinstance(start, int):
        return pl.ds(start, n)
    return pl.ds(pl.multiple_of(start, SEQ_TILE), n)


def _attn_prompt_kernel(lam_init, rb_ref, q_ref, k_ref, v_ref, g_ref, lamv_ref, sgc_ref, o_ref, vt_ref, s_ref):
    h = pl.program_id(1)
    t_pad, hw = q_ref.shape
    tk, tc = ATT_TILE, SEQ_TILE
    n_full, tail = divmod(t_pad, tk)
    far_bias = rb_ref[N_BUCKETS - 1, h]
    lam = _lambda(lamv_ref, lam_init)
    sgc = sgc_ref[...]

    ones_rows = (lax.broadcasted_iota(jnp.int32, (ONES_ROWS, tk), 0) == 0).astype(bf16)

    def fill_vt(j, n):
        vt_ref[j, 0:hw, 0:n] = v_ref[_rows(j * tk, n), :].astype(f32).T.astype(bf16)
        vt_ref[j, hw:hw + ONES_ROWS, :] = ones_rows

    def fill_body(j, carry):
        fill_vt(j, tk)
        return carry

    lax.fori_loop(0, n_full, fill_body, 0)
    if tail:
        fill_vt(n_full, tail)

    row = lax.broadcasted_iota(jnp.int32, (hw, tc), 0)

    def chain_q(q0, c):
        qt = q_ref[_rows(q0 + c * tc, tc), :].T
        return jnp.concatenate([jnp.where(row < HEAD_DIM, qt, 0.0), jnp.where(row >= HEAD_DIM, qt, 0.0)],
                               axis=1).astype(bf16)

    def update(state, s, off, vt):
        m, acc = state
        smax = jnp.max(s, axis=0, keepdims=True)
        m_new = jnp.maximum(m, smax if off is None else smax + off)
        alpha = jnp.exp(m - m_new)
        p = jnp.exp(s - (m_new if off is None else m_new - off))
        acc = alpha * acc + _dot(vt, p.astype(bf16))
        return m_new, acc

    def attend(q0, n_chain, n_prev, has_near):
        qq = [chain_q(q0, c) for c in range(n_chain)]
        states = tuple((jnp.full((1, 2 * tc), NEG_INF, f32), jnp.zeros((hw + ONES_ROWS, 2 * tc), f32))
                       for _ in range(n_chain))

        def logits(j, slot):
            kb = k_ref[_rows(j * tk, tk), :]
            for c in range(n_chain):
                s_ref[slot, c] = _dot(kb, qq[c])

        def far_step(j, slot, states):
            logits(j + 1, 1 - slot)
            vt = vt_ref[j]
            return tuple(update(states[c], s_ref[slot, c], far_bias, vt) for c in range(n_chain))

        def far_pair(j, st):
            return far_step(j + 1, 1, far_step(j, 0, st))

        if has_near:
            n_far = n_prev - 1
            n_quad = n_far >> 2
            pair = (n_far >> 1) & 1
            odd = n_far & 1
            logits(0, 0)
            states = _loop(n_quad, lambda i, st: far_pair(4 * i + 2, far_pair(4 * i, st)), states)
            states = _loop(pair, lambda i, st: far_pair(4 * n_quad, st), states)
            states = _loop(odd, lambda i, st: far_step(n_far - 1, 0, st), states)

        for c in range(n_chain):
            nk = (c + 1) * tc
            cols = slice(c * tc, (c + 1) * tc)
            g = g_ref[0, tk:tk + nk, cols]
            ki = lax.broadcasted_iota(jnp.int32, (nk, 2 * tc), 0)
            qi = c * tc + (lax.broadcasted_iota(jnp.int32, (nk, 2 * tc), 1) & (tc - 1))
            s = _dot(k_ref[_rows(q0, nk), :], qq[c]) + jnp.concatenate([g, g], axis=1)
            s = jnp.where(ki <= qi, s, NEG_INF)
            vt = vt_ref[n_prev][:, 0:nk]
            if has_near:
                g = g_ref[0, 0:tk, cols]
                s = jnp.concatenate([s_ref[odd, c] + jnp.concatenate([g, g], axis=1), s], axis=0)
                vt = jnp.concatenate([vt_ref[n_far], vt], axis=1)
            _, acc = update(states[c], s, None, vt)
            on = acc[0:hw] * (1.0 / acc[hw:hw + 1])
            ot = on[:, 0:tc] - lam * on[:, tc:2 * tc]
            ms = jnp.mean(ot * ot, axis=0, keepdims=True)
            ot = ot * lax.rsqrt(ms + SUBLN_EPS) * sgc * (1.0 - lam_init)
            o_ref[_rows(q0 + c * tc, tc), :] = ot.T

    def tile_body(p, carry):
        attend(p * tk, tk // tc, p, True)
        return carry

    attend(0, tk // tc, 0, False)
    lax.fori_loop(1, n_full, tile_body, 0)
    if tail:
        attend(n_full * tk, tail // tc, n_full, True)


def _attn_prompt(q, kb, vb, g_tiles, rel_bias, lamv, subln_g, n_seq, t_pad, lam_init):
    rows, d = q.shape
    hw = d // N_HEADS
    assert ATT_TILE >= FAR_DIST and ATT_TILE % SEQ_TILE == 0 and t_pad % SEQ_TILE == 0
    blk = lambda b, h: (b, h)
    sgc = subln_g.reshape(hw, 1)
    return pl.pallas_call(
        functools.partial(_attn_prompt_kernel, lam_init),
        grid=(n_seq, N_HEADS),
        in_specs=[pl.BlockSpec(memory_space=pltpu.SMEM),
                  pl.BlockSpec((t_pad, hw), blk), pl.BlockSpec((t_pad, hw), blk),
                  pl.BlockSpec((t_pad, hw), blk),
                  pl.BlockSpec((1,) + g_tiles.shape[1:], lambda b, h: (h, 0, 0)),
                  _const_spec(lamv.shape), _const_spec(sgc.shape)],
        out_specs=pl.BlockSpec((t_pad, hw), blk),
        out_shape=jax.ShapeDtypeStruct((n_seq * t_pad, d), f32),
        scratch_shapes=[pltpu.VMEM((pl.cdiv(t_pad, ATT_TILE), hw + ONES_ROWS, ATT_TILE), bf16),
                        pltpu.VMEM((2, ATT_TILE // SEQ_TILE, ATT_TILE, 2 * SEQ_TILE), f32)],
        compiler_params=_cparams(2),
        name="attn_prompt",
    )(rel_bias, q, kb, vb, g_tiles, lamv, sgc)


def _attn_sample_kernel(lam_init, n_pages, pt_ref, q_ref, kn_ref, vn_ref, bias_ref, lamv_ref, sg_ref,
                        *refs):
    k_refs = refs[:n_pages]
    v_refs = refs[n_pages:2 * n_pages]
    o_ref = refs[2 * n_pages]
    n_new, d = q_ref.shape
    n_grp = 2 * N_HEADS
    n_rows = n_grp * n_new
    hw = d // N_HEADS
    lam = _lambda(lamv_ref, lam_init)

    qt = jnp.concatenate([q_ref[...]] * n_grp, axis=0)
    r = lax.broadcasted_iota(jnp.int32, (n_rows, d), 0)
    col = lax.broadcasted_iota(jnp.int32, (n_rows, d), 1)
    row_grp = ((r // n_new) % N_HEADS) * 2 + r // (N_HEADS * n_new)
    wq = jnp.where(col // HEAD_DIM == row_grp, qt, 0.0).astype(bf16)

    pad = jnp.zeros((PAGE - n_new, d), f32)
    k_new = jnp.concatenate([kn_ref[...], pad], axis=0).astype(bf16)
    v_new = jnp.concatenate([vn_ref[...], pad], axis=0).astype(bf16)

    s = jnp.concatenate([_dot(wq, k_refs[j][0].astype(bf16)) for j in range(n_pages)]
                        + [_dot_nt(wq, k_new)], axis=1)
    s = s + bias_ref[...]
    past = n_pages * PAGE
    rr = lax.broadcasted_iota(jnp.int32, s.shape, 0)
    cc = lax.broadcasted_iota(jnp.int32, s.shape, 1)
    s = jnp.where(cc - past <= rr % n_new, s, NEG_INF)
    m = jnp.max(s, axis=-1, keepdims=True)
    p = jnp.exp(s - m)
    p = p / jnp.sum(p, axis=-1, keepdims=True)
    half = N_HEADS * n_new
    w = (p[0:half] - lam * p[half:2 * half]).astype(bf16)
    sg = sg_ref[...]
    heads = []
    for hh in range(N_HEADS):
        vh = jnp.concatenate([v_refs[j][0, pl.ds(hh, PAGE, stride=N_HEADS), :].astype(bf16)
                              for j in range(n_pages)] + [v_new[:, hh * hw:(hh + 1) * hw]], axis=0)
        out = _dot(w[hh * n_new:(hh + 1) * n_new, :], vh)
        heads.append(_rms(out, sg, SUBLN_EPS))
    o_ref[...] = jnp.concatenate(heads, axis=1) * (1.0 - lam_init)


def _attn_sample(q, k_new, v_new, cache_k, cache_v, page_table, bias_s, lamv, subln_g, row0, n_new, lam_init):
    n_batch, n_pages = page_table.shape
    d = q.shape[1]
    blk0 = row0 // n_new
    tok = pl.BlockSpec((n_new, d), lambda b, pt: (blk0 + b, 0))
    new = pl.BlockSpec((n_new, d), lambda b, pt: (b, 0))

    def page_spec(j):
        return pl.BlockSpec((1,) + cache_k.shape[1:], lambda b, pt: (pt[b, j], 0, 0))

    grid_spec = pltpu.PrefetchScalarGridSpec(
        num_scalar_prefetch=1,
        grid=(n_batch,),
        in_specs=[tok, new, new,
                  pl.BlockSpec(bias_s.shape, lambda b, pt: (0, 0), pipeline_mode=pl.Buffered(1)),
                  pl.BlockSpec(lamv.shape, lambda b, pt: (0, 0), pipeline_mode=pl.Buffered(1)),
                  pl.BlockSpec(subln_g.shape, lambda b, pt: (0, 0), pipeline_mode=pl.Buffered(1))]
        + [page_spec(j) for j in range(n_pages)] * 2,
        out_specs=new,
    )
    return pl.pallas_call(
        functools.partial(_attn_sample_kernel, lam_init, n_pages),
        grid_spec=grid_spec,
        out_shape=jax.ShapeDtypeStruct((n_batch * n_new, d), f32),
        compiler_params=_cparams(1),
        name="attn_sample",
    )(page_table, q, k_new, v_new, bias_s, lamv, subln_g,
      *([cache_k] * n_pages), *([cache_v] * n_pages))


def kernel(x_prompt, x_sample, cache_k, cache_v, page_table, state_conv, state_h, meta_tokens, norm_g, kv_norm_g, ffn_w_gate, ffn_w_up, ffn_w_down, lru_w_y, lru_w_x, lru_conv_w, lru_conv_b, lru_wa, lru_ba, lru_wx, lru_bx, lru_lambda, lru_w_out, attn_w_q, attn_w_k, attn_w_v, attn_w_o, attn_lambda_q1, attn_lambda_k1, attn_lambda_q2, attn_lambda_k2, attn_subln_g, rel_bias):
    n_seq, seq, d = x_prompt.shape
    n_batch, n_new, _ = x_sample.shape
    n_pool, page, _, _, _ = cache_k.shape
    n_pages = page_table.shape[1]
    depth = norm_g.shape[0]
    assert page == PAGE and n_new == SUBLANES and depth == 2 and lru_w_y.shape[0] == 1
    t_real = seq + N_META
    t_pad = -(-t_real // SEQ_TILE) * SEQ_TILE
    rows_p = n_seq * t_pad
    rows = rows_p + n_batch * n_new
    assert rows % ROW_TILE == 0 and rows_p % SEQ_TILE == 0 and (rows - rows_p) % SEQ_TILE == 0
    past_len = n_pages * PAGE

    meta = meta_tokens.astype(x_prompt.dtype)
    zpad = jnp.zeros((t_pad - t_real, d), x_prompt.dtype)
    parts = []
    for b in range(n_seq):
        parts += [meta, x_prompt[b], zpad]
    x = jnp.concatenate(parts + [x_sample.reshape(n_batch * n_new, d)], axis=0)

    wg_all, wu_all, wd_all = (w.astype(bf16) for w in (ffn_w_gate, ffn_w_up, ffn_w_down))

    def ffn(x, l, i):
        return _ffn(x, norm_g[l, 4 * i:4 * i + 2], wg_all, wu_all, wd_all, l, i)

    x = ffn(x, 0, 0)
    cb_rows = jnp.pad(state_conv[0], ((0, 0), (SUBLANES - (CONV_W - 1), 0), (0, 0))).reshape(n_batch * n_new, d)
    h0_rows = jnp.repeat(state_h[0], n_new, axis=0)
    vecs = jnp.stack([lru_conv_b[0], lru_ba[0], lru_bx[0], lru_lambda[0]])
    x, h_p, u_p, hs, u_s = _lru_layer(
        x, cb_rows, h0_rows, norm_g[0, 2:4], lru_w_y[0].astype(bf16), lru_w_x[0].astype(bf16), lru_conv_w[0],
        vecs, lru_wa[0].astype(bf16), lru_wx[0].astype(bf16), lru_w_out[0].astype(bf16), n_seq, t_pad, t_real)
    x = ffn(x, 0, 1)
    k_p, v_p, k_s, v_s, kb, vb = _kv(x, kv_norm_g[None], attn_w_k.astype(bf16), attn_w_v.astype(bf16),
                                     n_seq, t_pad, t_real)

    lam_init = 0.8 - 0.6 * math.exp(-0.3 * 1)
    x = ffn(x, 1, 0)
    q = _q_proj(x, norm_g[1, 2:3], attn_w_q[0].astype(bf16))
    g_tiles, bias_s = _bias_tables(rel_bias, past_len, n_new)
    bias_s = bias_s.reshape(2 * N_HEADS * n_new, past_len + PAGE)
    lamv = jnp.stack([attn_lambda_q1[0], attn_lambda_k1[0], attn_lambda_q2[0], attn_lambda_k2[0]])
    sg = attn_subln_g[0][None]
    o_p = _attn_prompt(q, kb, vb, g_tiles, rel_bias, lamv, sg, n_seq, t_pad, lam_init)
    cache_kt = jnp.transpose(cache_k, (0, 2, 3, 4, 1)).reshape(n_pool, d, PAGE)
    cache_vr = cache_v.reshape(n_pool, PAGE * N_HEADS, d // N_HEADS)
    o_s = _attn_sample(q, k_s, v_s, cache_kt, cache_vr, page_table, bias_s, lamv, sg, rows_p, n_new, lam_init)
    x = _proj_res([o_p, o_s], x, attn_w_o[0].astype(bf16), norm_g[1, 3:4])
    x = ffn(x, 1, 1)

    def prompt_rows(a, lo, hi):
        return jnp.stack([a[b * t_pad + lo:b * t_pad + hi] for b in range(n_seq)])

    def sample_rows(a):
        return a[rows_p:].reshape(n_batch, n_new, d)

    y_prompt = prompt_rows(x, N_META, t_real)
    y_sample = sample_rows(x)
    k_p = k_p.reshape(n_seq, t_real, N_HEADS, 2, HEAD_DIM)
    v_p = v_p.reshape(n_seq, t_real, N_HEADS, 2 * HEAD_DIM)
    last = (t_real - 1) % SUBLANES
    conv_p = u_p[:, last - (CONV_W - 2):last + 1][None]
    h_p = h_p.reshape(1, n_seq, d)
    k_s = k_s.reshape(n_batch, n_new, N_HEADS, 2, HEAD_DIM)
    v_s = v_s.reshape(n_batch, n_new, N_HEADS, 2 * HEAD_DIM)
    conv_s = u_s.reshape(n_batch, n_new, d)[:, n_new - (CONV_W - 1):][None]
    h_s = hs.reshape(n_batch, n_new, d)[:, n_new - 1][None]
    return (y_prompt, y_sample, k_p, v_p, conv_p, h_p, k_s, v_s, conv_s, h_s)
```

```python
import functools
import math

import numpy as np
import jax
import jax.numpy as jnp
from jax import lax
from jax.experimental import pallas as pl
from jax.experimental.pallas import tpu as pltpu

f32 = jnp.float32
bf16 = jnp.bfloat16

N_META = 16
N_HEADS = 8
HEAD_DIM = 64
CONV_W = 4
LRU_BLOCKS = 4
LRU_C = 8.0
N_BUCKETS = 32
MAX_DIST = 128
NORM_EPS = 1e-6
SUBLN_EPS = 1e-5
NEG_INF = -1e30
PAGE = 128

V7X_VMEM_BYTES = 64 * 1024 * 1024
VMEM_LIMIT = V7X_VMEM_BYTES - 8 * 1024 * 1024
SUBLANES = 8
LANES = 128

ROW_TILE = 512
SEQ_TILE = 256
ATT_TILE = 512
ONES_ROWS = 16


def _cparams(n_axes, flags=None):
    return pltpu.CompilerParams(dimension_semantics=("arbitrary",) * n_axes,
                                vmem_limit_bytes=VMEM_LIMIT, flags=flags)


def _const_spec(shape):
    nd = len(shape)
    return pl.BlockSpec(shape, lambda *_: (0,) * nd, pipeline_mode=pl.Buffered(1))


def _rms(x, g, eps):
    ms = jnp.mean(x * x, axis=-1, keepdims=True)
    return x * lax.rsqrt(ms + eps) * g


def _dot(a, b):
    return jnp.dot(a, b, preferred_element_type=f32)


def _dot_nt(a, b):
    return lax.dot_general(a, b, (((1,), (1,)), ((), ())), preferred_element_type=f32)


def _ffn_kernel(x_ref, g_ref, wg_ref, wu_ref, wd_ref, o_ref):
    x = x_ref[...]
    xn = _rms(x, g_ref[0:1, :], NORM_EPS).astype(bf16)
    hg = _dot(xn, wg_ref[...])
    hu = _dot(xn, wu_ref[...])
    act = (jax.nn.silu(hg) * hu).astype(bf16)
    y = _dot(act, wd_ref[...])
    o_ref[...] = x + 0.5 * _rms(y, g_ref[1:2, :], NORM_EPS)


def _ffn(x, g2, wg, wu, wd, l, i):
    rows, d = x.shape
    ffn = wg.shape[-1]

    def weight(shape):
        return pl.BlockSpec((None, None) + shape, lambda _: (l, i, 0, 0), pipeline_mode=pl.Buffered(1))

    row = pl.BlockSpec((ROW_TILE, d), lambda i: (i, 0))
    return pl.pallas_call(
        _ffn_kernel,
        grid=(rows // ROW_TILE,),
        in_specs=[row, _const_spec((2, d)), weight((d, ffn)), weight((d, ffn)), weight((ffn, d))],
        out_specs=row,
        out_shape=jax.ShapeDtypeStruct((rows, d), f32),
        compiler_params=_cparams(1),
        name="ffn",
    )(x, g2, wg, wu, wd)


def _ffn_out_kernel(n_p, x_ref, ap_ref, as_ref, wo_ref, g_ref, wg_ref, wu_ref, wd_ref, yp_ref, ys_ref):
    s = pl.program_id(0)
    a = jnp.where(s >= n_p, as_ref[...], ap_ref[...]).astype(bf16)
    x = x_ref[...] + _rms(_dot(a, wo_ref[...]), g_ref[0:1, :], NORM_EPS)
    xn = _rms(x, g_ref[1:2, :], NORM_EPS).astype(bf16)
    act = (jax.nn.silu(_dot(xn, wg_ref[...])) * _dot(xn, wu_ref[...])).astype(bf16)
    y = x + 0.5 * _rms(_dot(act, wd_ref[...]), g_ref[2:3, :], NORM_EPS)

    @pl.when(s < n_p)
    def _prompt():
        yp_ref[0] = y

    @pl.when(s >= n_p)
    def _sample():
        ys_ref[...] = y


def _ffn_out(x, o_p, o_s, wo, g3, wg, wu, wd, l, i, n_seq, t_pad, n_meta, seq):
    rows, d = x.shape
    ffn = wg.shape[-1]
    rows_p = n_seq * t_pad
    rows_s = rows - rows_p
    assert seq % ROW_TILE == 0 and rows_s % ROW_TILE == 0 and o_p.shape[0] == rows_p and o_s.shape[0] == rows_s
    per_seq = seq // ROW_TILE
    n_p = n_seq * per_seq

    def weight(shape):
        return pl.BlockSpec((None, None) + shape, lambda _: (l, i, 0, 0), pipeline_mode=pl.Buffered(1))

    def flat_start(s):
        prompt = (s // per_seq) * t_pad + n_meta + (s % per_seq) * ROW_TILE
        return jnp.where(s < n_p, prompt, rows_p + (s - n_p) * ROW_TILE)

    def kept_rows(limit):
        return pl.BlockSpec((pl.Element(ROW_TILE), pl.Element(d)),
                            lambda s: (pl.multiple_of(jnp.minimum(flat_start(s), limit - ROW_TILE), SUBLANES), 0))

    samp = pl.BlockSpec((ROW_TILE, d), lambda s: (jnp.maximum(s - n_p, 0), 0))
    prompt_out = pl.BlockSpec((1, ROW_TILE, d), lambda s: (jnp.minimum(s // per_seq, n_seq - 1),
                                                           jnp.where(s < n_p, s % per_seq, per_seq - 1), 0))
    return pl.pallas_call(
        functools.partial(_ffn_out_kernel, n_p),
        grid=(n_p + rows_s // ROW_TILE,),
        in_specs=[kept_rows(rows), kept_rows(rows_p), samp, _const_spec((d, d)), _const_spec((3, d)),
                  weight((d, ffn)), weight((d, ffn)), weight((ffn, d))],
        out_specs=[prompt_out, samp],
        out_shape=[jax.ShapeDtypeStruct((n_seq, seq, d), f32), jax.ShapeDtypeStruct((rows_s, d), f32)],
        compiler_params=_cparams(1),
        name="ffn_out",
    )(x, o_p, o_s, wo, g3, wg, wu, wd)


def _kv_kernel(n_prompt_tiles, x_ref, g_ref, wk_ref, wv_ref, kp_ref, vp_ref, ks_ref, vs_ref, kb_ref, vb_ref):
    s = pl.program_id(0)
    hk = _rms(x_ref[...], g_ref[...], NORM_EPS).astype(bf16)
    k = _dot(hk, wk_ref[...])
    v = _dot(hk, wv_ref[...])
    kb_ref[...] = k.astype(bf16)
    vb_ref[...] = v.astype(bf16)

    @pl.when(s < n_prompt_tiles)
    def _prompt():
        kp_ref[0] = k
        vp_ref[0] = v

    @pl.when(s >= n_prompt_tiles)
    def _sample():
        ks_ref[...] = k
        vs_ref[...] = v


def _kv(x, g, wk, wv, n_seq, t_pad, t_real):
    rows, d = x.shape
    n_seq_tiles = t_pad // SEQ_TILE
    n_prompt_tiles = n_seq * n_seq_tiles
    rows_s = rows - n_prompt_tiles * SEQ_TILE
    tile = pl.BlockSpec((SEQ_TILE, d), lambda s: (s, 0))
    prompt = pl.BlockSpec((1, SEQ_TILE, d),
                          lambda s: (jnp.minimum(s // n_seq_tiles, n_seq - 1),
                                     jnp.where(s < n_prompt_tiles, s % n_seq_tiles, n_seq_tiles - 1), 0))
    samp = pl.BlockSpec((SEQ_TILE, d), lambda s: (jnp.maximum(s - n_prompt_tiles, 0), 0))
    return pl.pallas_call(
        functools.partial(_kv_kernel, n_prompt_tiles),
        grid=(rows // SEQ_TILE,),
        in_specs=[tile, _const_spec((1, d)), _const_spec((d, d)), _const_spec((d, d))],
        out_specs=[prompt, prompt, samp, samp, tile, tile],
        out_shape=[jax.ShapeDtypeStruct((n_seq, t_real, d), f32)] * 2
        + [jax.ShapeDtypeStruct((rows_s, d), f32)] * 2
        + [jax.ShapeDtypeStruct((rows, d), bf16)] * 2,
        compiler_params=_cparams(1),
        name="kv_proj",
    )(x, g, wk, wv)


def _ffn_q_kernel(x_ref, g_ref, wg_ref, wu_ref, wd_ref, wq_ref, o_ref, q_ref):
    x = x_ref[...]
    xn = _rms(x, g_ref[0:1, :], NORM_EPS).astype(bf16)
    act = (jax.nn.silu(_dot(xn, wg_ref[...])) * _dot(xn, wu_ref[...])).astype(bf16)
    y = x + 0.5 * _rms(_dot(act, wd_ref[...]), g_ref[1:2, :], NORM_EPS)
    o_ref[...] = y
    h = _rms(y, g_ref[2:3, :], NORM_EPS).astype(bf16)
    q_ref[...] = _dot(h, wq_ref[...]) * (HEAD_DIM ** -0.5)


def _ffn_q(x, g3, wg, wu, wd, wq, l, i):
    rows, d = x.shape
    ffn = wg.shape[-1]

    def weight(shape):
        return pl.BlockSpec((None, None) + shape, lambda _: (l, i, 0, 0), pipeline_mode=pl.Buffered(1))

    row = pl.BlockSpec((ROW_TILE, d), lambda i: (i, 0))
    return pl.pallas_call(
        _ffn_q_kernel,
        grid=(rows // ROW_TILE,),
        in_specs=[row, _const_spec((3, d)), weight((d, ffn)), weight((d, ffn)), weight((ffn, d)),
                  _const_spec((d, d))],
        out_specs=[row, row],
        out_shape=[jax.ShapeDtypeStruct((rows, d), f32)] * 2,
        compiler_params=_cparams(1),
        name="ffn_q",
    )(x, g3, wg, wu, wd, wq)


def _log_sigmoid(x):
    return -(jnp.maximum(-x, 0.0) + jnp.log1p(jnp.exp(-jnp.abs(x))))


def _scan_groups(a, b):
    rows, c = a.shape
    a = a.reshape(rows // SUBLANES, SUBLANES, c)
    b = b.reshape(rows // SUBLANES, SUBLANES, c)
    t = lax.broadcasted_iota(jnp.int32, (1, SUBLANES, 1), 1)
    s = 1
    while s < SUBLANES:
        keep = t >= s
        a_s = jnp.where(keep, pltpu.roll(a, s, 1), 1.0)
        b_s = jnp.where(keep, pltpu.roll(b, s, 1), 0.0)
        b = a * b_s + b
        a = a * a_s
        s *= 2
    return a.reshape(rows, c), b.reshape(rows, c)


def _lru_kernel(n_seq_tiles, n_prompt_tiles, last_tile, last_row,
                x_ref, cb_ref, h0_ref, g_ref, wy_ref, wx_ref, cw_ref, vec_ref, wa_ref, wi_ref, wo_ref,
                xo_ref, hp_ref, cp_ref, hs_ref, us_ref, uext_ref, hcar_ref):
    s = pl.program_id(0)
    rows, c = x_ref.shape
    bw = c // LRU_BLOCKS
    conv_b, ba, bx, lam = (vec_ref[i:i + 1, :] for i in range(4))

    def project_in():
        hn = _rms(x_ref[...], g_ref[0:1, :], NORM_EPS).astype(bf16)
        return _dot(hn, wx_ref[...]), hn

    def project_out(h, hn):
        hy = (h * jax.nn.gelu(_dot(hn, wy_ref[...]))).astype(bf16)
        xo_ref[...] = x_ref[...] + _rms(_dot(hy, wo_ref[...]), g_ref[1:2, :], NORM_EPS)

    def gates_and_ab(uc):
        ucb = uc.astype(bf16)
        rp = jnp.concatenate([_dot(ucb[:, n * bw:(n + 1) * bw], wa_ref[n]) for n in range(LRU_BLOCKS)], axis=1)
        ip = jnp.concatenate([_dot(ucb[:, n * bw:(n + 1) * bw], wi_ref[n]) for n in range(LRU_BLOCKS)], axis=1)
        r = jax.nn.sigmoid(rp + ba)
        i = jax.nn.sigmoid(ip + bx)
        log_a = LRU_C * r * _log_sigmoid(lam)
        a = jnp.exp(log_a)
        b = jnp.sqrt(-jnp.tanh(log_a) * (a * a + 1.0)) * (i * uc)
        return a, b

    @pl.when(s < n_prompt_tiles)
    def _prompt():
        i = jnp.where(s >= n_seq_tiles, s - n_seq_tiles, s)

        @pl.when(i == 0)
        def _():
            uext_ref[0:SUBLANES, :] = jnp.zeros((SUBLANES, c), f32)
            hcar_ref[...] = jnp.zeros_like(hcar_ref)

        u, hn = project_in()
        uext_ref[SUBLANES:SUBLANES + rows, :] = u
        uc = conv_b + cw_ref[CONV_W - 1:CONV_W, :] * u
        for j in range(1, CONV_W):
            uc = uc + cw_ref[CONV_W - 1 - j:CONV_W - j, :] * uext_ref[SUBLANES - j:SUBLANES - j + rows, :]
        a, b = gates_and_ab(uc)
        a, b = _scan_groups(a, b)
        carry = hcar_ref[...]
        groups = []
        for g in range(rows // SUBLANES):
            hg = a[g * SUBLANES:(g + 1) * SUBLANES] * carry + b[g * SUBLANES:(g + 1) * SUBLANES]
            carry = hg[SUBLANES - 1:SUBLANES]
            groups.append(hg)
        h = jnp.concatenate(groups, axis=0)
        project_out(h, hn)
        uext_ref[0:SUBLANES, :] = u[rows - SUBLANES:rows]
        hcar_ref[...] = carry

        @pl.when(i == last_tile)
        def _():
            grp = last_row // SUBLANES * SUBLANES
            hp_ref[0] = h[last_row:last_row + 1]
            cp_ref[0] = u[grp:grp + SUBLANES]

    @pl.when(s >= n_prompt_tiles)
    def _sample():
        u, hn = project_in()
        t = lax.broadcasted_iota(jnp.int32, (rows, 1), 0) % SUBLANES
        cb = cb_ref[...]
        uc = conv_b + cw_ref[CONV_W - 1:CONV_W, :] * u
        for j in range(1, CONV_W):
            uj = jnp.where(t >= j, pltpu.roll(u, j, 0), pltpu.roll(cb, rows - SUBLANES + j, 0))
            uc = uc + cw_ref[CONV_W - 1 - j:CONV_W - j, :] * uj
        a, b = gates_and_ab(uc)
        a, b = _scan_groups(a, b)
        h = a * h0_ref[...] + b
        project_out(h, hn)
        hs_ref[...] = h
        us_ref[...] = u


def _lru_layer(x, cb_rows, h0_rows, g2, wy, wx, conv_w, vecs, wa, wi, wo, n_seq, t_pad, t_real):
    rows, c = x.shape
    n_seq_tiles = t_pad // SEQ_TILE
    n_prompt_tiles = n_seq * n_seq_tiles
    n_tiles = rows // SEQ_TILE
    rows_s = rows - n_prompt_tiles * SEQ_TILE
    bw = c // LRU_BLOCKS
    last_row = (t_real - 1) % SEQ_TILE
    assert last_row % SUBLANES >= CONV_W - 2
    tile = pl.BlockSpec((SEQ_TILE, c), lambda s: (s, 0))
    samp = pl.BlockSpec((SEQ_TILE, c), lambda s: (jnp.maximum(s - n_prompt_tiles, 0), 0))

    def per_seq(n):
        return pl.BlockSpec((1, n, c), lambda s: (jnp.minimum(s // n_seq_tiles, n_seq - 1), 0, 0))

    kern = functools.partial(_lru_kernel, n_seq_tiles, n_prompt_tiles, (t_real - 1) // SEQ_TILE, last_row)
    return pl.pallas_call(
        kern,
        grid=(n_tiles,),
        in_specs=[tile, samp, samp, _const_spec((2, c)), _const_spec((c, c)), _const_spec((c, c)),
                  _const_spec((CONV_W, c)), _const_spec((4, c)),
                  _const_spec((LRU_BLOCKS, bw, bw)), _const_spec((LRU_BLOCKS, bw, bw)), _const_spec((c, c))],
        out_specs=[tile, per_seq(1), per_seq(SUBLANES), samp, samp],
        out_shape=[jax.ShapeDtypeStruct((rows, c), f32),
                   jax.ShapeDtypeStruct((n_seq, 1, c), f32),
                   jax.ShapeDtypeStruct((n_seq, SUBLANES, c), f32),
                   jax.ShapeDtypeStruct((rows_s, c), f32),
                   jax.ShapeDtypeStruct((rows_s, c), f32)],
        scratch_shapes=[pltpu.VMEM((SUBLANES + SEQ_TILE, c), f32), pltpu.VMEM((1, c), f32)],
        compiler_params=_cparams(1),
        name="lru_layer",
    )(x, cb_rows, h0_rows, g2, wy, wx, conv_w, vecs, wa, wi, wo)


def _bucket_last_dist():
    n = np.arange(0, 4 * MAX_DIST, dtype=np.int32)
    max_exact = N_BUCKETS // 2
    nf = np.maximum(n, 1).astype(np.float32)
    large = max_exact + (np.log(nf / np.float32(max_exact)) / np.float32(math.log(MAX_DIST / max_exact))
                         * np.float32(N_BUCKETS - max_exact)).astype(np.int32)
    bucket = np.where(n < max_exact, n, np.minimum(large, N_BUCKETS - 1))
    assert np.all(np.diff(bucket) >= 0)
    last = [int(n[bucket == b].max()) if np.any(bucket == b) else None for b in range(N_BUCKETS)]
    assert last[N_BUCKETS - 1] == n[-1]
    return last


_BUCKET_LAST = _bucket_last_dist()
FAR_DIST = max(d for d in _BUCKET_LAST[:-1] if d is not None) + 1


def _bias_of_dist(dist, rb_ref, h):
    val = jnp.full(dist.shape, rb_ref[N_BUCKETS - 1, h], f32)
    for b in range(N_BUCKETS - 2, -1, -1):
        if _BUCKET_LAST[b] is not None:
            val = jnp.where(dist <= _BUCKET_LAST[b], rb_ref[b, h], val)
    return val


def _bias_kernel(past_len, rb_ref, g_ref, bs_ref):
    h = pl.program_id(0)
    _, nk, tq = g_ref.shape
    ki = lax.broadcasted_iota(jnp.int32, (nk, tq), 0)
    qi = lax.broadcasted_iota(jnp.int32, (nk, tq), 1)
    g_ref[0] = _bias_of_dist(qi - ki + (nk - tq), rb_ref, h)
    _, _, nq, wk = bs_ref.shape
    q = lax.broadcasted_iota(jnp.int32, (nq, wk), 0)
    col = lax.broadcasted_iota(jnp.int32, (nq, wk), 1)
    bias = _bias_of_dist(past_len + q - col, rb_ref, h)
    bs_ref[0, 0] = bias
    bs_ref[1, 0] = bias


def _bias_tables(rel_bias, past_len, n_new):
    g_shape = (N_HEADS, 2 * ATT_TILE, ATT_TILE)
    bs_shape = (2, N_HEADS, n_new, past_len + PAGE)
    return pl.pallas_call(
        functools.partial(_bias_kernel, past_len),
        grid=(N_HEADS,),
        in_specs=[pl.BlockSpec(memory_space=pltpu.SMEM)],
        out_specs=[pl.BlockSpec((1,) + g_shape[1:], lambda h: (h, 0, 0)),
                   pl.BlockSpec((2, 1) + bs_shape[2:], lambda h: (0, h, 0, 0))],
        out_shape=[jax.ShapeDtypeStruct(g_shape, f32), jax.ShapeDtypeStruct(bs_shape, f32)],
        compiler_params=_cparams(1),
        name="rel_bias_tables",
    )(rel_bias)


def _lambda(lamv_ref, lam_init):
    l1 = jnp.sum(lamv_ref[0:1, :] * lamv_ref[1:2, :], axis=-1, keepdims=True)
    l2 = jnp.sum(lamv_ref[2:3, :] * lamv_ref[3:4, :], axis=-1, keepdims=True)
    return jnp.exp(l1) - jnp.exp(l2) + lam_init


def _loop(n, body, init):
    if isinstance(n, int) and n == 0:
        return init
    return lax.fori_loop(0, n, body, init)


def _rows(start, n):
    if isinstance(start, int):
        return pl.ds(start, n)
    return pl.ds(pl.multiple_of(start, SEQ_TILE), n)


def _attn_prompt_kernel(lam_init, rb_ref, q_ref, k_ref, v_ref, g_ref, lamv_ref, sgc_ref, o_ref, vt_ref, s_ref):
    h = pl.program_id(1)
    t_pad, hw = q_ref.shape
    tk, tc = ATT_TILE, SEQ_TILE
    n_full, tail = divmod(t_pad, tk)
    far_bias = rb_ref[N_BUCKETS - 1, h]
    lam = _lambda(lamv_ref, lam_init)
    sgc = sgc_ref[...]

    ones_rows = (lax.broadcasted_iota(jnp.int32, (ONES_ROWS, tk), 0) == 0).astype(bf16)

    def fill_vt(j, n):
        vt_ref[j, 0:hw, 0:n] = v_ref[_rows(j * tk, n), :].astype(f32).T.astype(bf16)
        vt_ref[j, hw:hw + ONES_ROWS, :] = ones_rows

    def fill_body(j, carry):
        fill_vt(j, tk)
        return carry

    lax.fori_loop(0, n_full, fill_body, 0)
    if tail:
        fill_vt(n_full, tail)

    row = lax.broadcasted_iota(jnp.int32, (hw, tc), 0)

    def chain_q(q0, c):
        qt = q_ref[_rows(q0 + c * tc, tc), :].T
        return jnp.concatenate([jnp.where(row < HEAD_DIM, qt, 0.0), jnp.where(row >= HEAD_DIM, qt, 0.0)],
                               axis=1).astype(bf16)

    def update(state, s, off, vt):
        m, acc = state
        smax = jnp.max(s, axis=0, keepdims=True)
        m_new = jnp.maximum(m, smax if off is None else smax + off)
        alpha = jnp.exp(m - m_new)
        p = jnp.exp(s - (m_new if off is None else m_new - off))
        acc = alpha * acc + _dot(vt, p.astype(bf16))
        return m_new, acc

    def attend(q0, n_chain, n_prev, has_near):
        qq = [chain_q(q0, c) for c in range(n_chain)]
        states = tuple((jnp.full((1, 2 * tc), NEG_INF, f32), jnp.zeros((hw + ONES_ROWS, 2 * tc), f32))
                       for _ in range(n_chain))

        def logits(j, slot):
            kb = k_ref[_rows(j * tk, tk), :]
            for c in range(n_chain):
                s_ref[slot, c] = _dot(kb, qq[c])

        def far_step(j, slot, states):
            logits(j + 1, 1 - slot)
            vt = vt_ref[j]
            return tuple(update(states[c], s_ref[slot, c], far_bias, vt) for c in range(n_chain))

        def far_pair(j, st):
            return far_step(j + 1, 1, far_step(j, 0, st))

        if has_near:
            n_far = n_prev - 1
            n_quad = n_far >> 2
            pair = (n_far >> 1) & 1
            odd = n_far & 1
            logits(0, 0)
            states = _loop(n_quad, lambda i, st: far_pair(4 * i + 2, far_pair(4 * i, st)), states)
            states = _loop(pair, lambda i, st: far_pair(4 * n_quad, st), states)
            states = _loop(odd, lambda i, st: far_step(n_far - 1, 0, st), states)

        diag = [_dot(k_ref[_rows(q0, (c + 1) * tc), :], qq[c]) for c in range(n_chain)]
        for c in range(n_chain):
            nk = (c + 1) * tc
            cols = slice(c * tc, (c + 1) * tc)
            g = g_ref[0, tk:tk + nk, cols]
            ki = lax.broadcasted_iota(jnp.int32, (nk, 2 * tc), 0)
            qi = c * tc + (lax.broadcasted_iota(jnp.int32, (nk, 2 * tc), 1) & (tc - 1))
            s = jnp.where(ki <= qi, diag[c] + jnp.concatenate([g, g], axis=1), NEG_INF)
            vt = vt_ref[n_prev][:, 0:nk]
            if has_near:
                g = g_ref[0, 0:tk, cols]
                s = jnp.concatenate([s_ref[odd, c] + jnp.concatenate([g, g], axis=1), s], axis=0)
                vt = jnp.concatenate([vt_ref[n_far], vt], axis=1)
            _, acc = update(states[c], s, None, vt)
            on = acc[0:hw] * (1.0 / acc[hw:hw + 1])
            ot = on[:, 0:tc] - lam * on[:, tc:2 * tc]
            ms = jnp.mean(ot * ot, axis=0, keepdims=True)
            ot = ot * lax.rsqrt(ms + SUBLN_EPS) * sgc * (1.0 - lam_init)
            o_ref[_rows(q0 + c * tc, tc), :] = ot.T

    def tile_body(p, carry):
        attend(p * tk, tk // tc, p, True)
        return carry

    attend(0, tk // tc, 0, False)
    lax.fori_loop(1, n_full, tile_body, 0)
    if tail:
        attend(n_full * tk, tail // tc, n_full, True)


def _attn_prompt(q, kb, vb, g_tiles, rel_bias, lamv, subln_g, n_seq, t_pad, lam_init):
    rows, d = q.shape
    hw = d // N_HEADS
    assert ATT_TILE >= FAR_DIST and ATT_TILE % SEQ_TILE == 0 and t_pad % SEQ_TILE == 0
    blk = lambda b, h: (b, h)
    sgc = subln_g.reshape(hw, 1)
    return pl.pallas_call(
        functools.partial(_attn_prompt_kernel, lam_init),
        grid=(n_seq, N_HEADS),
        in_specs=[pl.BlockSpec(memory_space=pltpu.SMEM),
                  pl.BlockSpec((t_pad, hw), blk), pl.BlockSpec((t_pad, hw), blk),
                  pl.BlockSpec((t_pad, hw), blk),
                  pl.BlockSpec((1,) + g_tiles.shape[1:], lambda b, h: (h, 0, 0)),
                  _const_spec(lamv.shape), _const_spec(sgc.shape)],
        out_specs=pl.BlockSpec((t_pad, hw), blk),
        out_shape=jax.ShapeDtypeStruct((n_seq * t_pad, d), f32),
        scratch_shapes=[pltpu.VMEM((pl.cdiv(t_pad, ATT_TILE), hw + ONES_ROWS, ATT_TILE), bf16),
                        pltpu.VMEM((2, ATT_TILE // SEQ_TILE, ATT_TILE, 2 * SEQ_TILE), f32)],
        compiler_params=_cparams(2),
        name="attn_prompt",
    )(rel_bias, q, kb, vb, g_tiles, lamv, sgc)


def _attn_sample_kernel(lam_init, n_pages, pt_ref, q_ref, kn_ref, vn_ref, bias_ref, lamv_ref, sg_ref,
                        *refs):
    k_refs = refs[:n_pages]
    v_refs = refs[n_pages:2 * n_pages]
    o_ref = refs[2 * n_pages]
    n_new, d = q_ref.shape
    n_grp = 2 * N_HEADS
    n_rows = n_grp * n_new
    hw = d // N_HEADS
    lam = _lambda(lamv_ref, lam_init)

    qt = jnp.concatenate([q_ref[...]] * n_grp, axis=0)
    r = lax.broadcasted_iota(jnp.int32, (n_rows, d), 0)
    col = lax.broadcasted_iota(jnp.int32, (n_rows, d), 1)
    row_grp = ((r // n_new) % N_HEADS) * 2 + r // (N_HEADS * n_new)
    wq = jnp.where(col // HEAD_DIM == row_grp, qt, 0.0).astype(bf16)

    pad = jnp.zeros((PAGE - n_new, d), f32)
    k_new = jnp.concatenate([kn_ref[...], pad], axis=0).astype(bf16)
    v_new = jnp.concatenate([vn_ref[...], pad], axis=0).astype(bf16)

    s = jnp.concatenate([_dot(wq, k_refs[j][0].astype(bf16)) for j in range(n_pages)]
                        + [_dot_nt(wq, k_new)], axis=1)
    s = s + bias_ref[...]
    past = n_pages * PAGE
    rr = lax.broadcasted_iota(jnp.int32, s.shape, 0)
    cc = lax.broadcasted_iota(jnp.int32, s.shape, 1)
    s = jnp.where(cc - past <= rr % n_new, s, NEG_INF)
    m = jnp.max(s, axis=-1, keepdims=True)
    p = jnp.exp(s - m)
    p = p / jnp.sum(p, axis=-1, keepdims=True)
    half = N_HEADS * n_new
    w = (p[0:half] - lam * p[half:2 * half]).astype(bf16)
    sg = sg_ref[...]
    heads = []
    for hh in range(N_HEADS):
        vh = jnp.concatenate([v_refs[j][0, pl.ds(hh, PAGE, stride=N_HEADS), :].astype(bf16)
                              for j in range(n_pages)] + [v_new[:, hh * hw:(hh + 1) * hw]], axis=0)
        out = _dot(w[hh * n_new:(hh + 1) * n_new, :], vh)
        heads.append(_rms(out, sg, SUBLN_EPS))
    o_ref[...] = jnp.concatenate(heads, axis=1) * (1.0 - lam_init)


def _attn_sample(q, k_new, v_new, cache_k, cache_v, page_table, bias_s, lamv, subln_g, row0, n_new, lam_init):
    n_batch, n_pages = page_table.shape
    d = q.shape[1]
    blk0 = row0 // n_new
    tok = pl.BlockSpec((n_new, d), lambda b, pt: (blk0 + b, 0))
    new = pl.BlockSpec((n_new, d), lambda b, pt: (b, 0))

    def page_spec(j):
        return pl.BlockSpec((1,) + cache_k.shape[1:], lambda b, pt: (pt[b, j], 0, 0))

    grid_spec = pltpu.PrefetchScalarGridSpec(
        num_scalar_prefetch=1,
        grid=(n_batch,),
        in_specs=[tok, new, new,
                  pl.BlockSpec(bias_s.shape, lambda b, pt: (0, 0), pipeline_mode=pl.Buffered(1)),
                  pl.BlockSpec(lamv.shape, lambda b, pt: (0, 0), pipeline_mode=pl.Buffered(1)),
                  pl.BlockSpec(subln_g.shape, lambda b, pt: (0, 0), pipeline_mode=pl.Buffered(1))]
        + [page_spec(j) for j in range(n_pages)] * 2,
        out_specs=new,
    )
    return pl.pallas_call(
        functools.partial(_attn_sample_kernel, lam_init, n_pages),
        grid_spec=grid_spec,
        out_shape=jax.ShapeDtypeStruct((n_batch * n_new, d), f32),
        compiler_params=_cparams(1),
        name="attn_sample",
    )(page_table, q, k_new, v_new, bias_s, lamv, subln_g,
      *([cache_k] * n_pages), *([cache_v] * n_pages))


def kernel(x_prompt, x_sample, cache_k, cache_v, page_table, state_conv, state_h, meta_tokens, norm_g, kv_norm_g, ffn_w_gate, ffn_w_up, ffn_w_down, lru_w_y, lru_w_x, lru_conv_w, lru_conv_b, lru_wa, lru_ba, lru_wx, lru_bx, lru_lambda, lru_w_out, attn_w_q, attn_w_k, attn_w_v, attn_w_o, attn_lambda_q1, attn_lambda_k1, attn_lambda_q2, attn_lambda_k2, attn_subln_g, rel_bias):
    n_seq, seq, d = x_prompt.shape
    n_batch, n_new, _ = x_sample.shape
    n_pool, page, _, _, _ = cache_k.shape
    n_pages = page_table.shape[1]
    depth = norm_g.shape[0]
    assert page == PAGE and n_new == SUBLANES and depth == 2 and lru_w_y.shape[0] == 1
    t_real = seq + N_META
    t_pad = -(-t_real // SEQ_TILE) * SEQ_TILE
    rows_p = n_seq * t_pad
    rows = rows_p + n_batch * n_new
    assert rows % ROW_TILE == 0 and rows_p % SEQ_TILE == 0 and (rows - rows_p) % SEQ_TILE == 0
    past_len = n_pages * PAGE

    meta = meta_tokens.astype(x_prompt.dtype)
    zpad = jnp.zeros((t_pad - t_real, d), x_prompt.dtype)
    parts = []
    for b in range(n_seq):
        parts += [meta, x_prompt[b], zpad]
    x = jnp.concatenate(parts + [x_sample.reshape(n_batch * n_new, d)], axis=0)

    wg_all, wu_all, wd_all = (w.astype(bf16) for w in (ffn_w_gate, ffn_w_up, ffn_w_down))

    def ffn(x, l, i):
        return _ffn(x, norm_g[l, 4 * i:4 * i + 2], wg_all, wu_all, wd_all, l, i)

    x = ffn(x, 0, 0)
    cb_rows = jnp.pad(state_conv[0], ((0, 0), (SUBLANES - (CONV_W - 1), 0), (0, 0))).reshape(n_batch * n_new, d)
    h0_rows = jnp.repeat(state_h[0], n_new, axis=0)
    vecs = jnp.stack([lru_conv_b[0], lru_ba[0], lru_bx[0], lru_lambda[0]])
    x, h_p, u_p, hs, u_s = _lru_layer(
        x, cb_rows, h0_rows, norm_g[0, 2:4], lru_w_y[0].astype(bf16), lru_w_x[0].astype(bf16), lru_conv_w[0],
        vecs, lru_wa[0].astype(bf16), lru_wx[0].astype(bf16), lru_w_out[0].astype(bf16), n_seq, t_pad, t_real)
    x = ffn(x, 0, 1)
    k_p, v_p, k_s, v_s, kb, vb = _kv(x, kv_norm_g[None], attn_w_k.astype(bf16), attn_w_v.astype(bf16),
                                     n_seq, t_pad, t_real)

    lam_init = 0.8 - 0.6 * math.exp(-0.3 * 1)
    x, q = _ffn_q(x, norm_g[1, 0:3], wg_all, wu_all, wd_all, attn_w_q[0].astype(bf16), 1, 0)
    g_tiles, bias_s = _bias_tables(rel_bias, past_len, n_new)
    bias_s = bias_s.reshape(2 * N_HEADS * n_new, past_len + PAGE)
    lamv = jnp.stack([attn_lambda_q1[0], attn_lambda_k1[0], attn_lambda_q2[0], attn_lambda_k2[0]])
    sg = attn_subln_g[0][None]
    o_p = _attn_prompt(q, kb, vb, g_tiles, rel_bias, lamv, sg, n_seq, t_pad, lam_init)
    cache_kt = jnp.transpose(cache_k, (0, 2, 3, 4, 1)).reshape(n_pool, d, PAGE)
    cache_vr = cache_v.reshape(n_pool, PAGE * N_HEADS, d // N_HEADS)
    o_s = _attn_sample(q, k_s, v_s, cache_kt, cache_vr, page_table, bias_s, lamv, sg, rows_p, n_new, lam_init)
    y_prompt, y_s = _ffn_out(x, o_p, o_s, attn_w_o[0].astype(bf16), norm_g[1, 3:6], wg_all, wu_all, wd_all, 1, 1,
                             n_seq, t_pad, N_META, seq)


    y_sample = y_s.reshape(n_batch, n_new, d)
    k_p = k_p.reshape(n_seq, t_real, N_HEADS, 2, HEAD_DIM)
    v_p = v_p.reshape(n_seq, t_real, N_HEADS, 2 * HEAD_DIM)
    last = (t_real - 1) % SUBLANES
    conv_p = u_p[:, last - (CONV_W - 2):last + 1][None]
    h_p = h_p.reshape(1, n_seq, d)
    k_s = k_s.reshape(n_batch, n_new, N_HEADS, 2, HEAD_DIM)
    v_s = v_s.reshape(n_batch, n_new, N_HEADS, 2 * HEAD_DIM)
    conv_s = u_s.reshape(n_batch, n_new, d)[:, n_new - (CONV_W - 1):][None]
    h_s = hs.reshape(n_batch, n_new, d)[:, n_new - 1][None]
    return (y_prompt, y_sample, k_p, v_p, conv_p, h_p, k_s, v_s, conv_s, h_s)
```

```python
import functools
import math

import numpy as np
import jax
import jax.numpy as jnp
from jax import lax
from jax.experimental import pallas as pl
from jax.experimental.pallas import tpu as pltpu

f32 = jnp.float32
bf16 = jnp.bfloat16

N_META = 16
N_HEADS = 8
HEAD_DIM = 64
CONV_W = 4
LRU_BLOCKS = 4
LRU_C = 8.0
N_BUCKETS = 32
MAX_DIST = 128
NORM_EPS = 1e-6
SUBLN_EPS = 1e-5
NEG_INF = -1e30
PAGE = 128

V7X_VMEM_BYTES = 64 * 1024 * 1024
VMEM_LIMIT = V7X_VMEM_BYTES - 8 * 1024 * 1024
SUBLANES = 8
LANES = 128

ROW_TILE = 512
SEQ_TILE = 256
ATT_TILE = 512
ONES_ROWS = 16


def _cparams(n_axes, flags=None):
    return pltpu.CompilerParams(dimension_semantics=("arbitrary",) * n_axes,
                                vmem_limit_bytes=VMEM_LIMIT, flags=flags)


def _const_spec(shape):
    nd = len(shape)
    return pl.BlockSpec(shape, lambda *_: (0,) * nd, pipeline_mode=pl.Buffered(1))


def _rms(x, g, eps):
    ms = jnp.mean(x * x, axis=-1, keepdims=True)
    return x * lax.rsqrt(ms + eps) * g


def _dot(a, b):
    return jnp.dot(a, b, preferred_element_type=f32)


def _dot_nt(a, b):
    return lax.dot_general(a, b, (((1,), (1,)), ((), ())), preferred_element_type=f32)


def _ffn_kernel(x_ref, g_ref, wg_ref, wu_ref, wd_ref, o_ref):
    x = x_ref[...]
    xn = _rms(x, g_ref[0:1, :], NORM_EPS).astype(bf16)
    hg = _dot(xn, wg_ref[...])
    hu = _dot(xn, wu_ref[...])
    act = (jax.nn.silu(hg) * hu).astype(bf16)
    y = _dot(act, wd_ref[...])
    o_ref[...] = x + 0.5 * _rms(y, g_ref[1:2, :], NORM_EPS)


def _ffn(x, g2, wg, wu, wd, l, i):
    rows, d = x.shape
    ffn = wg.shape[-1]

    def weight(shape):
        return pl.BlockSpec((None, None) + shape, lambda _: (l, i, 0, 0), pipeline_mode=pl.Buffered(1))

    row = pl.BlockSpec((ROW_TILE, d), lambda i: (i, 0))
    return pl.pallas_call(
        _ffn_kernel,
        grid=(rows // ROW_TILE,),
        in_specs=[row, _const_spec((2, d)), weight((d, ffn)), weight((d, ffn)), weight((ffn, d))],
        out_specs=row,
        out_shape=jax.ShapeDtypeStruct((rows, d), f32),
        compiler_params=_cparams(1),
        name="ffn",
    )(x, g2, wg, wu, wd)


def _ffn_out_kernel(n_p, x_ref, ap_ref, as_ref, wo_ref, g_ref, wg_ref, wu_ref, wd_ref, yp_ref, ys_ref):
    s = pl.program_id(0)
    a = jnp.where(s >= n_p, as_ref[...], ap_ref[...]).astype(bf16)
    x = x_ref[...] + _rms(_dot(a, wo_ref[...]), g_ref[0:1, :], NORM_EPS)
    xn = _rms(x, g_ref[1:2, :], NORM_EPS).astype(bf16)
    act = (jax.nn.silu(_dot(xn, wg_ref[...])) * _dot(xn, wu_ref[...])).astype(bf16)
    y = x + 0.5 * _rms(_dot(act, wd_ref[...]), g_ref[2:3, :], NORM_EPS)

    @pl.when(s < n_p)
    def _prompt():
        yp_ref[0] = y

    @pl.when(s >= n_p)
    def _sample():
        ys_ref[...] = y


def _ffn_out(x, o_p, o_s, wo, g3, wg, wu, wd, l, i, n_seq, t_pad, n_meta, seq):
    rows, d = x.shape
    ffn = wg.shape[-1]
    rows_p = n_seq * t_pad
    rows_s = rows - rows_p
    assert seq % ROW_TILE == 0 and rows_s % ROW_TILE == 0 and o_p.shape[0] == rows_p and o_s.shape[0] == rows_s
    per_seq = seq // ROW_TILE
    n_p = n_seq * per_seq

    def weight(shape):
        return pl.BlockSpec((None, None) + shape, lambda _: (l, i, 0, 0), pipeline_mode=pl.Buffered(1))

    def flat_start(s):
        prompt = (s // per_seq) * t_pad + n_meta + (s % per_seq) * ROW_TILE
        return jnp.where(s < n_p, prompt, rows_p + (s - n_p) * ROW_TILE)

    def kept_rows(limit):
        return pl.BlockSpec((pl.Element(ROW_TILE), pl.Element(d)),
                            lambda s: (pl.multiple_of(jnp.minimum(flat_start(s), limit - ROW_TILE), SUBLANES), 0))

    samp = pl.BlockSpec((ROW_TILE, d), lambda s: (jnp.maximum(s - n_p, 0), 0))
    prompt_out = pl.BlockSpec((1, ROW_TILE, d), lambda s: (jnp.minimum(s // per_seq, n_seq - 1),
                                                           jnp.where(s < n_p, s % per_seq, per_seq - 1), 0))
    return pl.pallas_call(
        functools.partial(_ffn_out_kernel, n_p),
        grid=(n_p + rows_s // ROW_TILE,),
        in_specs=[kept_rows(rows), kept_rows(rows_p), samp, _const_spec((d, d)), _const_spec((3, d)),
                  weight((d, ffn)), weight((d, ffn)), weight((ffn, d))],
        out_specs=[prompt_out, samp],
        out_shape=[jax.ShapeDtypeStruct((n_seq, seq, d), f32), jax.ShapeDtypeStruct((rows_s, d), f32)],
        compiler_params=_cparams(1),
        name="ffn_out",
    )(x, o_p, o_s, wo, g3, wg, wu, wd)


def _kv_kernel(n_prompt_tiles, x_ref, g_ref, wk_ref, wv_ref, kp_ref, vp_ref, ks_ref, vs_ref, kb_ref, vb_ref):
    s = pl.program_id(0)
    hk = _rms(x_ref[...], g_ref[...], NORM_EPS).astype(bf16)
    k = _dot(hk, wk_ref[...])
    v = _dot(hk, wv_ref[...])
    kb_ref[...] = k.astype(bf16)
    vb_ref[...] = v.astype(bf16)

    @pl.when(s < n_prompt_tiles)
    def _prompt():
        kp_ref[0] = k
        vp_ref[0] = v

    @pl.when(s >= n_prompt_tiles)
    def _sample():
        ks_ref[...] = k
        vs_ref[...] = v


def _kv(x, g, wk, wv, n_seq, t_pad, t_real):
    rows, d = x.shape
    n_seq_tiles = t_pad // SEQ_TILE
    n_prompt_tiles = n_seq * n_seq_tiles
    rows_s = rows - n_prompt_tiles * SEQ_TILE
    tile = pl.BlockSpec((SEQ_TILE, d), lambda s: (s, 0))
    prompt = pl.BlockSpec((1, SEQ_TILE, d),
                          lambda s: (jnp.minimum(s // n_seq_tiles, n_seq - 1),
                                     jnp.where(s < n_prompt_tiles, s % n_seq_tiles, n_seq_tiles - 1), 0))
    samp = pl.BlockSpec((SEQ_TILE, d), lambda s: (jnp.maximum(s - n_prompt_tiles, 0), 0))
    return pl.pallas_call(
        functools.partial(_kv_kernel, n_prompt_tiles),
        grid=(rows // SEQ_TILE,),
        in_specs=[tile, _const_spec((1, d)), _const_spec((d, d)), _const_spec((d, d))],
        out_specs=[prompt, prompt, samp, samp, tile, tile],
        out_shape=[jax.ShapeDtypeStruct((n_seq, t_real, d), f32)] * 2
        + [jax.ShapeDtypeStruct((rows_s, d), f32)] * 2
        + [jax.ShapeDtypeStruct((rows, d), bf16)] * 2,
        compiler_params=_cparams(1),
        name="kv_proj",
    )(x, g, wk, wv)


def _ffn_q_kernel(x_ref, g_ref, wg_ref, wu_ref, wd_ref, wq_ref, o_ref, q_ref):
    x = x_ref[...]
    xn = _rms(x, g_ref[0:1, :], NORM_EPS).astype(bf16)
    act = (jax.nn.silu(_dot(xn, wg_ref[...])) * _dot(xn, wu_ref[...])).astype(bf16)
    y = x + 0.5 * _rms(_dot(act, wd_ref[...]), g_ref[1:2, :], NORM_EPS)
    o_ref[...] = y
    h = _rms(y, g_ref[2:3, :], NORM_EPS).astype(bf16)
    q_ref[...] = _dot(h, wq_ref[...]) * (HEAD_DIM ** -0.5)


def _ffn_q(x, g3, wg, wu, wd, wq, l, i):
    rows, d = x.shape
    ffn = wg.shape[-1]

    def weight(shape):
        return pl.BlockSpec((None, None) + shape, lambda _: (l, i, 0, 0), pipeline_mode=pl.Buffered(1))

    row = pl.BlockSpec((ROW_TILE, d), lambda i: (i, 0))
    return pl.pallas_call(
        _ffn_q_kernel,
        grid=(rows // ROW_TILE,),
        in_specs=[row, _const_spec((3, d)), weight((d, ffn)), weight((d, ffn)), weight((ffn, d)),
                  _const_spec((d, d))],
        out_specs=[row, row],
        out_shape=[jax.ShapeDtypeStruct((rows, d), f32)] * 2,
        compiler_params=_cparams(1),
        name="ffn_q",
    )(x, g3, wg, wu, wd, wq)


def _log_sigmoid(x):
    return -(jnp.maximum(-x, 0.0) + jnp.log1p(jnp.exp(-jnp.abs(x))))


def _sigmoid(x):
    return 0.5 * jnp.tanh(0.5 * x) + 0.5


def _scan_groups(a, b):
    rows, c = a.shape
    a = a.reshape(rows // SUBLANES, SUBLANES, c)
    b = b.reshape(rows // SUBLANES, SUBLANES, c)
    t = lax.broadcasted_iota(jnp.int32, (1, SUBLANES, 1), 1)
    s = 1
    while s < SUBLANES:
        keep = t >= s
        a_s = jnp.where(keep, pltpu.roll(a, s, 1), 1.0)
        b_s = jnp.where(keep, pltpu.roll(b, s, 1), 0.0)
        b = a * b_s + b
        a = a * a_s
        s *= 2
    return a.reshape(rows, c), b.reshape(rows, c)


def _lru_kernel(n_seq_tiles, n_prompt_tiles, last_tile, last_row,
                x_ref, cb_ref, h0_ref, g_ref, wy_ref, wx_ref, cw_ref, vec_ref, wa_ref, wi_ref, wo_ref,
                xo_ref, hp_ref, cp_ref, hs_ref, us_ref, uext_ref, hcar_ref):
    s = pl.program_id(0)
    rows, c = x_ref.shape
    bw = c // LRU_BLOCKS
    conv_b, ba, bx, lam = (vec_ref[i:i + 1, :] for i in range(4))

    def project_in():
        hn = _rms(x_ref[...], g_ref[0:1, :], NORM_EPS).astype(bf16)
        return _dot(hn, wx_ref[...]), hn

    def project_out(h, hn):
        hy = (h * jax.nn.gelu(_dot(hn, wy_ref[...]))).astype(bf16)
        xo_ref[...] = x_ref[...] + _rms(_dot(hy, wo_ref[...]), g_ref[1:2, :], NORM_EPS)

    def gates_and_ab(uc):
        ucb = uc.astype(bf16)
        rp = jnp.concatenate([_dot(ucb[:, n * bw:(n + 1) * bw], wa_ref[n]) for n in range(LRU_BLOCKS)], axis=1)
        ip = jnp.concatenate([_dot(ucb[:, n * bw:(n + 1) * bw], wi_ref[n]) for n in range(LRU_BLOCKS)], axis=1)
        r = jax.nn.sigmoid(rp + ba)
        i = _sigmoid(ip + bx)
        log_a = LRU_C * r * _log_sigmoid(lam)
        a = jnp.exp(log_a)
        y = -jnp.tanh(log_a) * (a * a + 1.0)
        b = jnp.where(y > 0.0, y * lax.rsqrt(y), 0.0) * (i * uc)
        return a, b

    @pl.when(s < n_prompt_tiles)
    def _prompt():
        i = jnp.where(s >= n_seq_tiles, s - n_seq_tiles, s)

        @pl.when(i == 0)
        def _():
            uext_ref[0:SUBLANES, :] = jnp.zeros((SUBLANES, c), f32)
            hcar_ref[...] = jnp.zeros_like(hcar_ref)

        u, hn = project_in()
        uext_ref[SUBLANES:SUBLANES + rows, :] = u
        uc = conv_b + cw_ref[CONV_W - 1:CONV_W, :] * u
        for j in range(1, CONV_W):
            uc = uc + cw_ref[CONV_W - 1 - j:CONV_W - j, :] * uext_ref[SUBLANES - j:SUBLANES - j + rows, :]
        a, b = gates_and_ab(uc)
        a, b = _scan_groups(a, b)
        carry = hcar_ref[...]
        groups = []
        for g in range(rows // SUBLANES):
            hg = a[g * SUBLANES:(g + 1) * SUBLANES] * carry + b[g * SUBLANES:(g + 1) * SUBLANES]
            carry = hg[SUBLANES - 1:SUBLANES]
            groups.append(hg)
        h = jnp.concatenate(groups, axis=0)
        project_out(h, hn)
        uext_ref[0:SUBLANES, :] = u[rows - SUBLANES:rows]
        hcar_ref[...] = carry

        @pl.when(i == last_tile)
        def _():
            grp = last_row // SUBLANES * SUBLANES
            hp_ref[0] = h[last_row:last_row + 1]
            cp_ref[0] = u[grp:grp + SUBLANES]

    @pl.when(s >= n_prompt_tiles)
    def _sample():
        u, hn = project_in()
        t = lax.broadcasted_iota(jnp.int32, (rows, 1), 0) % SUBLANES
        cb = cb_ref[...]
        uc = conv_b + cw_ref[CONV_W - 1:CONV_W, :] * u
        for j in range(1, CONV_W):
            uj = jnp.where(t >= j, pltpu.roll(u, j, 0), pltpu.roll(cb, rows - SUBLANES + j, 0))
            uc = uc + cw_ref[CONV_W - 1 - j:CONV_W - j, :] * uj
        a, b = gates_and_ab(uc)
        a, b = _scan_groups(a, b)
        h = a * h0_ref[...] + b
        project_out(h, hn)
        hs_ref[...] = h
        us_ref[...] = u


def _lru_layer(x, cb_rows, h0_rows, g2, wy, wx, conv_w, vecs, wa, wi, wo, n_seq, t_pad, t_real):
    rows, c = x.shape
    n_seq_tiles = t_pad // SEQ_TILE
    n_prompt_tiles = n_seq * n_seq_tiles
    n_tiles = rows // SEQ_TILE
    rows_s = rows - n_prompt_tiles * SEQ_TILE
    bw = c // LRU_BLOCKS
    last_row = (t_real - 1) % SEQ_TILE
    assert last_row % SUBLANES >= CONV_W - 2
    tile = pl.BlockSpec((SEQ_TILE, c), lambda s: (s, 0))
    samp = pl.BlockSpec((SEQ_TILE, c), lambda s: (jnp.maximum(s - n_prompt_tiles, 0), 0))

    def per_seq(n):
        return pl.BlockSpec((1, n, c), lambda s: (jnp.minimum(s // n_seq_tiles, n_seq - 1), 0, 0))

    kern = functools.partial(_lru_kernel, n_seq_tiles, n_prompt_tiles, (t_real - 1) // SEQ_TILE, last_row)
    return pl.pallas_call(
        kern,
        grid=(n_tiles,),
        in_specs=[tile, samp, samp, _const_spec((2, c)), _const_spec((c, c)), _const_spec((c, c)),
                  _const_spec((CONV_W, c)), _const_spec((4, c)),
                  _const_spec((LRU_BLOCKS, bw, bw)), _const_spec((LRU_BLOCKS, bw, bw)), _const_spec((c, c))],
        out_specs=[tile, per_seq(1), per_seq(SUBLANES), samp, samp],
        out_shape=[jax.ShapeDtypeStruct((rows, c), f32),
                   jax.ShapeDtypeStruct((n_seq, 1, c), f32),
                   jax.ShapeDtypeStruct((n_seq, SUBLANES, c), f32),
                   jax.ShapeDtypeStruct((rows_s, c), f32),
                   jax.ShapeDtypeStruct((rows_s, c), f32)],
        scratch_shapes=[pltpu.VMEM((SUBLANES + SEQ_TILE, c), f32), pltpu.VMEM((1, c), f32)],
        compiler_params=_cparams(1),
        name="lru_layer",
    )(x, cb_rows, h0_rows, g2, wy, wx, conv_w, vecs, wa, wi, wo)


def _bucket_last_dist():
    n = np.arange(0, 4 * MAX_DIST, dtype=np.int32)
    max_exact = N_BUCKETS // 2
    nf = np.maximum(n, 1).astype(np.float32)
    large = max_exact + (np.log(nf / np.float32(max_exact)) / np.float32(math.log(MAX_DIST / max_exact))
                         * np.float32(N_BUCKETS - max_exact)).astype(np.int32)
    bucket = np.where(n < max_exact, n, np.minimum(large, N_BUCKETS - 1))
    assert np.all(np.diff(bucket) >= 0)
    last = [int(n[bucket == b].max()) if np.any(bucket == b) else None for b in range(N_BUCKETS)]
    assert last[N_BUCKETS - 1] == n[-1]
    return last


_BUCKET_LAST = _bucket_last_dist()
FAR_DIST = max(d for d in _BUCKET_LAST[:-1] if d is not None) + 1


def _bias_of_dist(dist, rb_ref, h):
    val = jnp.full(dist.shape, rb_ref[N_BUCKETS - 1, h], f32)
    for b in range(N_BUCKETS - 2, -1, -1):
        if _BUCKET_LAST[b] is not None:
            val = jnp.where(dist <= _BUCKET_LAST[b], rb_ref[b, h], val)
    return val


def _bias_kernel(past_len, rb_ref, g_ref, bs_ref):
    h = pl.program_id(0)
    _, nk, tq = g_ref.shape
    ki = lax.broadcasted_iota(jnp.int32, (nk, tq), 0)
    qi = lax.broadcasted_iota(jnp.int32, (nk, tq), 1)
    g_ref[0] = _bias_of_dist(qi - ki + (nk - tq), rb_ref, h)
    _, _, nq, wk = bs_ref.shape
    q = lax.broadcasted_iota(jnp.int32, (nq, wk), 0)
    col = lax.broadcasted_iota(jnp.int32, (nq, wk), 1)
    bias = _bias_of_dist(past_len + q - col, rb_ref, h)
    bs_ref[0, 0] = bias
    bs_ref[1, 0] = bias


def _bias_tables(rel_bias, past_len, n_new):
    g_shape = (N_HEADS, 2 * ATT_TILE, ATT_TILE)
    bs_shape = (2, N_HEADS, n_new, past_len + PAGE)
    return pl.pallas_call(
        functools.partial(_bias_kernel, past_len),
        grid=(N_HEADS,),
        in_specs=[pl.BlockSpec(memory_space=pltpu.SMEM)],
        out_specs=[pl.BlockSpec((1,) + g_shape[1:], lambda h: (h, 0, 0)),
                   pl.BlockSpec((2, 1) + bs_shape[2:], lambda h: (0, h, 0, 0))],
        out_shape=[jax.ShapeDtypeStruct(g_shape, f32), jax.ShapeDtypeStruct(bs_shape, f32)],
        compiler_params=_cparams(1),
        name="rel_bias_tables",
    )(rel_bias)


def _lambda(lamv_ref, lam_init):
    l1 = jnp.sum(lamv_ref[0:1, :] * lamv_ref[1:2, :], axis=-1, keepdims=True)
    l2 = jnp.sum(lamv_ref[2:3, :] * lamv_ref[3:4, :], axis=-1, keepdims=True)
    return jnp.exp(l1) - jnp.exp(l2) + lam_init


def _loop(n, body, init):
    if isinstance(n, int) and n == 0:
        return init
    return lax.fori_loop(0, n, body, init)


def _rows(start, n):
    if isinstance(start, int):
        return pl.ds(start, n)
    return pl.ds(pl.multiple_of(start, SEQ_TILE), n)


def _attn_prompt_kernel(lam_init, rb_ref, q_ref, k_ref, v_ref, g_ref, lamv_ref, sgc_ref, o_ref, vt_ref, s_ref):
    h = pl.program_id(1)
    t_pad, hw = q_ref.shape
    tk, tc = ATT_TILE, SEQ_TILE
    n_full, tail = divmod(t_pad, tk)
    far_bias = rb_ref[N_BUCKETS - 1, h]
    lam = _lambda(lamv_ref, lam_init)
    sgc = sgc_ref[...]

    ones_rows = (lax.broadcasted_iota(jnp.int32, (ONES_ROWS, tk), 0) == 0).astype(bf16)

    def fill_vt(j, n):
        vt_ref[j, 0:hw, 0:n] = v_ref[_rows(j * tk, n), :].astype(f32).T.astype(bf16)
        vt_ref[j, hw:hw + ONES_ROWS, :] = ones_rows

    def fill_body(j, carry):
        fill_vt(j, tk)
        return carry

    lax.fori_loop(0, n_full, fill_body, 0)
    if tail:
        fill_vt(n_full, tail)

    row = lax.broadcasted_iota(jnp.int32, (hw, tc), 0)

    def chain_q(q0, c):
        qt = q_ref[_rows(q0 + c * tc, tc), :].T
        return jnp.concatenate([jnp.where(row < HEAD_DIM, qt, 0.0), jnp.where(row >= HEAD_DIM, qt, 0.0)],
                               axis=1).astype(bf16)

    def update(state, s, off, vt):
        m, acc = state
        smax = jnp.max(s, axis=0, keepdims=True)
        m_new = jnp.maximum(m, smax if off is None else smax + off)
        alpha = jnp.exp(m - m_new)
        p = jnp.exp(s - (m_new if off is None else m_new - off))
        acc = alpha * acc + _dot(vt, p.astype(bf16))
        return m_new, acc

    def init_states(n_chain):
        return tuple((jnp.full((1, 2 * tc), NEG_INF, f32), jnp.zeros((hw + ONES_ROWS, 2 * tc), f32))
                     for _ in range(n_chain))

    def far_part(q0, n_chain, n_prev):
        qq = [chain_q(q0, c) for c in range(n_chain)]

        def logits(j, slot):
            kb = k_ref[_rows(j * tk, tk), :]
            for c in range(n_chain):
                s_ref[slot, c] = _dot(kb, qq[c])

        def far_step(j, slot, states):
            logits(j + 1, 1 - slot)
            vt = vt_ref[j]
            return tuple(update(states[c], s_ref[slot, c], far_bias, vt) for c in range(n_chain))

        def far_pair(j, st):
            return far_step(j + 1, 1, far_step(j, 0, st))

        n_far = n_prev - 1
        n_quad = n_far >> 2
        logits(0, 0)
        states = _loop(n_quad, lambda i, st: far_pair(4 * i + 2, far_pair(4 * i, st)), init_states(n_chain))
        states = _loop((n_far >> 1) & 1, lambda i, st: far_pair(4 * n_quad, st), states)
        return _loop(n_far & 1, lambda i, st: far_step(n_far - 1, 0, st), states)

    def close_part(q0, n_chain, n_prev, has_near, states):
        qq = [chain_q(q0, c) for c in range(n_chain)]
        for c in range(n_chain):
            nk = (c + 1) * tc
            cols = slice(c * tc, (c + 1) * tc)
            g = g_ref[0, tk:tk + nk, cols]
            ki = lax.broadcasted_iota(jnp.int32, (nk, 2 * tc), 0)
            qi = c * tc + (lax.broadcasted_iota(jnp.int32, (nk, 2 * tc), 1) & (tc - 1))
            s = _dot(k_ref[_rows(q0, nk), :], qq[c]) + jnp.concatenate([g, g], axis=1)
            s = jnp.where(ki <= qi, s, NEG_INF)
            vt = vt_ref[n_prev][:, 0:nk]
            if has_near:
                g = g_ref[0, 0:tk, cols]
                near = s_ref[(n_prev - 1) & 1, c] + jnp.concatenate([g, g], axis=1)
                s = jnp.concatenate([near, s], axis=0)
                vt = jnp.concatenate([vt_ref[n_prev - 1], vt], axis=1)
            _, acc = update(states[c], s, None, vt)
            on = acc[0:hw] * (1.0 / acc[hw:hw + 1])
            ot = on[:, 0:tc] - lam * on[:, tc:2 * tc]
            ms = jnp.mean(ot * ot, axis=0, keepdims=True)
            ot = ot * lax.rsqrt(ms + SUBLN_EPS) * sgc * (1.0 - lam_init)
            o_ref[_rows(q0 + c * tc, tc), :] = ot.T

    n_chain = tk // tc
    assert n_full >= 2
    close_part(0, n_chain, 0, False, init_states(n_chain))
    states = far_part(tk, n_chain, 1)

    def tile_body(p, states):
        close_part((p - 1) * tk, n_chain, p - 1, True, states)
        return far_part(p * tk, n_chain, p)

    states = lax.fori_loop(2, n_full, tile_body, states)
    close_part((n_full - 1) * tk, n_chain, n_full - 1, True, states)
    if tail:
        states = far_part(n_full * tk, tail // tc, n_full)
        close_part(n_full * tk, tail // tc, n_full, True, states)


def _attn_prompt(q, kb, vb, g_tiles, rel_bias, lamv, subln_g, n_seq, t_pad, lam_init):
    rows, d = q.shape
    hw = d // N_HEADS
    assert ATT_TILE >= FAR_DIST and ATT_TILE % SEQ_TILE == 0 and t_pad % SEQ_TILE == 0
    blk = lambda b, h: (b, h)
    sgc = subln_g.reshape(hw, 1)
    return pl.pallas_call(
        functools.partial(_attn_prompt_kernel, lam_init),
        grid=(n_seq, N_HEADS),
        in_specs=[pl.BlockSpec(memory_space=pltpu.SMEM),
                  pl.BlockSpec((t_pad, hw), blk), pl.BlockSpec((t_pad, hw), blk),
                  pl.BlockSpec((t_pad, hw), blk),
                  pl.BlockSpec((1,) + g_tiles.shape[1:], lambda b, h: (h, 0, 0)),
                  _const_spec(lamv.shape), _const_spec(sgc.shape)],
        out_specs=pl.BlockSpec((t_pad, hw), blk),
        out_shape=jax.ShapeDtypeStruct((n_seq * t_pad, d), f32),
        scratch_shapes=[pltpu.VMEM((pl.cdiv(t_pad, ATT_TILE), hw + ONES_ROWS, ATT_TILE), bf16),
                        pltpu.VMEM((2, ATT_TILE // SEQ_TILE, ATT_TILE, 2 * SEQ_TILE), f32)],
        compiler_params=_cparams(2),
        name="attn_prompt",
    )(rel_bias, q, kb, vb, g_tiles, lamv, sgc)


def _attn_sample_kernel(lam_init, n_pages, pt_ref, q_ref, kn_ref, vn_ref, bias_ref, lamv_ref, sg_ref,
                        *refs):
    k_refs = refs[:n_pages]
    v_refs = refs[n_pages:2 * n_pages]
    o_ref = refs[2 * n_pages]
    n_new, d = q_ref.shape
    n_grp = 2 * N_HEADS
    n_rows = n_grp * n_new
    hw = d // N_HEADS
    lam = _lambda(lamv_ref, lam_init)

    qt = jnp.concatenate([q_ref[...]] * n_grp, axis=0)
    r = lax.broadcasted_iota(jnp.int32, (n_rows, d), 0)
    col = lax.broadcasted_iota(jnp.int32, (n_rows, d), 1)
    row_grp = ((r // n_new) % N_HEADS) * 2 + r // (N_HEADS * n_new)
    wq = jnp.where(col // HEAD_DIM == row_grp, qt, 0.0).astype(bf16)

    pad = jnp.zeros((PAGE - n_new, d), f32)
    k_new = jnp.concatenate([kn_ref[...], pad], axis=0).astype(bf16)
    v_new = jnp.concatenate([vn_ref[...], pad], axis=0).astype(bf16)

    s = jnp.concatenate([_dot(wq, k_refs[j][0].astype(bf16)) for j in range(n_pages)]
                        + [_dot_nt(wq, k_new)], axis=1)
    s = s + bias_ref[...]
    past = n_pages * PAGE
    rr = lax.broadcasted_iota(jnp.int32, s.shape, 0)
    cc = lax.broadcasted_iota(jnp.int32, s.shape, 1)
    s = jnp.where(cc - past <= rr % n_new, s, NEG_INF)
    m = jnp.max(s, axis=-1, keepdims=True)
    p = jnp.exp(s - m)
    p = p / jnp.sum(p, axis=-1, keepdims=True)
    half = N_HEADS * n_new
    w = (p[0:half] - lam * p[half:2 * half]).astype(bf16)
    sg = sg_ref[...]
    heads = []
    for hh in range(N_HEADS):
        vh = jnp.concatenate([v_refs[j][0, pl.ds(hh, PAGE, stride=N_HEADS), :].astype(bf16)
                              for j in range(n_pages)] + [v_new[:, hh * hw:(hh + 1) * hw]], axis=0)
        out = _dot(w[hh * n_new:(hh + 1) * n_new, :], vh)
        heads.append(_rms(out, sg, SUBLN_EPS))
    o_ref[...] = jnp.concatenate(heads, axis=1) * (1.0 - lam_init)


def _attn_sample(q, k_new, v_new, cache_k, cache_v, page_table, bias_s, lamv, subln_g, row0, n_new, lam_init):
    n_batch, n_pages = page_table.shape
    d = q.shape[1]
    blk0 = row0 // n_new
    tok = pl.BlockSpec((n_new, d), lambda b, pt: (blk0 + b, 0))
    new = pl.BlockSpec((n_new, d), lambda b, pt: (b, 0))

    def page_spec(j):
        return pl.BlockSpec((1,) + cache_k.shape[1:], lambda b, pt: (pt[b, j], 0, 0))

    grid_spec = pltpu.PrefetchScalarGridSpec(
        num_scalar_prefetch=1,
        grid=(n_batch,),
        in_specs=[tok, new, new,
                  pl.BlockSpec(bias_s.shape, lambda b, pt: (0, 0), pipeline_mode=pl.Buffered(1)),
                  pl.BlockSpec(lamv.shape, lambda b, pt: (0, 0), pipeline_mode=pl.Buffered(1)),
                  pl.BlockSpec(subln_g.shape, lambda b, pt: (0, 0), pipeline_mode=pl.Buffered(1))]
        + [page_spec(j) for j in range(n_pages)] * 2,
        out_specs=new,
    )
    return pl.pallas_call(
        functools.partial(_attn_sample_kernel, lam_init, n_pages),
        grid_spec=grid_spec,
        out_shape=jax.ShapeDtypeStruct((n_batch * n_new, d), f32),
        compiler_params=_cparams(1),
        name="attn_sample",
    )(page_table, q, k_new, v_new, bias_s, lamv, subln_g,
      *([cache_k] * n_pages), *([cache_v] * n_pages))


def kernel(x_prompt, x_sample, cache_k, cache_v, page_table, state_conv, state_h, meta_tokens, norm_g, kv_norm_g, ffn_w_gate, ffn_w_up, ffn_w_down, lru_w_y, lru_w_x, lru_conv_w, lru_conv_b, lru_wa, lru_ba, lru_wx, lru_bx, lru_lambda, lru_w_out, attn_w_q, attn_w_k, attn_w_v, attn_w_o, attn_lambda_q1, attn_lambda_k1, attn_lambda_q2, attn_lambda_k2, attn_subln_g, rel_bias):
    n_seq, seq, d = x_prompt.shape
    n_batch, n_new, _ = x_sample.shape
    n_pool, page, _, _, _ = cache_k.shape
    n_pages = page_table.shape[1]
    depth = norm_g.shape[0]
    assert page == PAGE and n_new == SUBLANES and depth == 2 and lru_w_y.shape[0] == 1
    t_real = seq + N_META
    t_pad = -(-t_real // SEQ_TILE) * SEQ_TILE
    rows_p = n_seq * t_pad
    rows = rows_p + n_batch * n_new
    assert rows % ROW_TILE == 0 and rows_p % SEQ_TILE == 0 and (rows - rows_p) % SEQ_TILE == 0
    past_len = n_pages * PAGE

    meta = meta_tokens.astype(x_prompt.dtype)
    zpad = jnp.zeros((t_pad - t_real, d), x_prompt.dtype)
    parts = []
    for b in range(n_seq):
        parts += [meta, x_prompt[b], zpad]
    x = jnp.concatenate(parts + [x_sample.reshape(n_batch * n_new, d)], axis=0)

    wg_all, wu_all, wd_all = (w.astype(bf16) for w in (ffn_w_gate, ffn_w_up, ffn_w_down))

    def ffn(x, l, i):
        return _ffn(x, norm_g[l, 4 * i:4 * i + 2], wg_all, wu_all, wd_all, l, i)

    x = ffn(x, 0, 0)
    cb_rows = jnp.pad(state_conv[0], ((0, 0), (SUBLANES - (CONV_W - 1), 0), (0, 0))).reshape(n_batch * n_new, d)
    h0_rows = jnp.repeat(state_h[0], n_new, axis=0)
    vecs = jnp.stack([lru_conv_b[0], lru_ba[0], lru_bx[0], lru_lambda[0]])
    x, h_p, u_p, hs, u_s = _lru_layer(
        x, cb_rows, h0_rows, norm_g[0, 2:4], lru_w_y[0].astype(bf16), lru_w_x[0].astype(bf16), lru_conv_w[0],
        vecs, lru_wa[0].astype(bf16), lru_wx[0].astype(bf16), lru_w_out[0].astype(bf16), n_seq, t_pad, t_real)
    x = ffn(x, 0, 1)
    k_p, v_p, k_s, v_s, kb, vb = _kv(x, kv_norm_g[None], attn_w_k.astype(bf16), attn_w_v.astype(bf16),
                                     n_seq, t_pad, t_real)

    lam_init = 0.8 - 0.6 * math.exp(-0.3 * 1)
    x, q = _ffn_q(x, norm_g[1, 0:3], wg_all, wu_all, wd_all, attn_w_q[0].astype(bf16), 1, 0)
    g_tiles, bias_s = _bias_tables(rel_bias, past_len, n_new)
    bias_s = bias_s.reshape(2 * N_HEADS * n_new, past_len + PAGE)
    lamv = jnp.stack([attn_lambda_q1[0], attn_lambda_k1[0], attn_lambda_q2[0], attn_lambda_k2[0]])
    sg = attn_subln_g[0][None]
    o_p = _attn_prompt(q, kb, vb, g_tiles, rel_bias, lamv, sg, n_seq, t_pad, lam_init)
    cache_kt = jnp.transpose(cache_k, (0, 2, 3, 4, 1)).reshape(n_pool, d, PAGE)
    cache_vr = cache_v.reshape(n_pool, PAGE * N_HEADS, d // N_HEADS)
    o_s = _attn_sample(q, k_s, v_s, cache_kt, cache_vr, page_table, bias_s, lamv, sg, rows_p, n_new, lam_init)
    y_prompt, y_s = _ffn_out(x, o_p, o_s, attn_w_o[0].astype(bf16), norm_g[1, 3:6], wg_all, wu_all, wd_all, 1, 1,
                             n_seq, t_pad, N_META, seq)


    y_sample = y_s.reshape(n_batch, n_new, d)
    k_p = k_p.reshape(n_seq, t_real, N_HEADS, 2, HEAD_DIM)
    v_p = v_p.reshape(n_seq, t_real, N_HEADS, 2 * HEAD_DIM)
    last = (t_real - 1) % SUBLANES
    conv_p = u_p[:, last - (CONV_W - 2):last + 1][None]
    h_p = h_p.reshape(1, n_seq, d)
    k_s = k_s.reshape(n_batch, n_new, N_HEADS, 2, HEAD_DIM)
    v_s = v_s.reshape(n_batch, n_new, N_HEADS, 2 * HEAD_DIM)
    conv_s = u_s.reshape(n_batch, n_new, d)[:, n_new - (CONV_W - 1):][None]
    h_s = hs.reshape(n_batch, n_new, d)[:, n_new - 1][None]
    return (y_prompt, y_sample, k_p, v_p, conv_p, h_p, k_s, v_s, conv_s, h_s)
```

```python
import functools
import math

import numpy as np
import jax
import jax.numpy as jnp
from jax import lax
from jax.experimental import pallas as pl
from jax.experimental.pallas import tpu as pltpu

f32 = jnp.float32
bf16 = jnp.bfloat16

N_META = 16
N_HEADS = 8
HEAD_DIM = 64
CONV_W = 4
LRU_BLOCKS = 4
LRU_C = 8.0
N_BUCKETS = 32
MAX_DIST = 128
NORM_EPS = 1e-6
SUBLN_EPS = 1e-5
NEG_INF = -1e30
PAGE = 128

V7X_VMEM_BYTES = 64 * 1024 * 1024
VMEM_LIMIT = V7X_VMEM_BYTES - 8 * 1024 * 1024
SUBLANES = 8
LANES = 128

ROW_TILE = 512
SEQ_TILE = 256
ATT_TILE = 512
ATT_CHAIN = 256
ATT_CHAINS = 2
ONES_ROWS = 16


def _cparams(n_axes):
    return pltpu.CompilerParams(dimension_semantics=("arbitrary",) * n_axes, vmem_limit_bytes=VMEM_LIMIT)


def _const_spec(shape):
    nd = len(shape)
    return pl.BlockSpec(shape, lambda *_: (0,) * nd, pipeline_mode=pl.Buffered(1))


def _rms(x, g, eps):
    ms = jnp.mean(x * x, axis=-1, keepdims=True)
    return x * lax.rsqrt(ms + eps) * g


def _dot(a, b):
    return jnp.dot(a, b, preferred_element_type=f32)


def _dot_nt(a, b):
    return lax.dot_general(a, b, (((1,), (1,)), ((), ())), preferred_element_type=f32)


def _ffn_kernel(x_ref, g_ref, wg_ref, wu_ref, wd_ref, o_ref):
    x = x_ref[...]
    xn = _rms(x, g_ref[0:1, :], NORM_EPS).astype(bf16)
    hg = _dot(xn, wg_ref[...])
    hu = _dot(xn, wu_ref[...])
    act = (jax.nn.silu(hg) * hu).astype(bf16)
    y = _dot(act, wd_ref[...])
    o_ref[...] = x + 0.5 * _rms(y, g_ref[1:2, :], NORM_EPS)


def _ffn(x, g2, wg, wu, wd, l, i):
    rows, d = x.shape
    ffn = wg.shape[-1]

    def weight(shape):
        return pl.BlockSpec((None, None) + shape, lambda _: (l, i, 0, 0), pipeline_mode=pl.Buffered(1))

    row = pl.BlockSpec((ROW_TILE, d), lambda i: (i, 0))
    return pl.pallas_call(
        _ffn_kernel,
        grid=(rows // ROW_TILE,),
        in_specs=[row, _const_spec((2, d)), weight((d, ffn)), weight((d, ffn)), weight((ffn, d))],
        out_specs=row,
        out_shape=jax.ShapeDtypeStruct((rows, d), f32),
        compiler_params=_cparams(1),
        name="ffn",
    )(x, g2, wg, wu, wd)


def _ffn_out_kernel(n_p, x_ref, ap_ref, as_ref, wo_ref, g_ref, wg_ref, wu_ref, wd_ref, yp_ref, ys_ref):
    s = pl.program_id(0)
    a = jnp.where(s >= n_p, as_ref[...].astype(bf16), ap_ref[...])
    x = x_ref[...] + _rms(_dot(a, wo_ref[...]), g_ref[0:1, :], NORM_EPS)
    xn = _rms(x, g_ref[1:2, :], NORM_EPS).astype(bf16)
    act = (jax.nn.silu(_dot(xn, wg_ref[...])) * _dot(xn, wu_ref[...])).astype(bf16)
    y = x + 0.5 * _rms(_dot(act, wd_ref[...]), g_ref[2:3, :], NORM_EPS)

    @pl.when(s < n_p)
    def _prompt():
        yp_ref[0] = y

    @pl.when(s >= n_p)
    def _sample():
        ys_ref[...] = y


def _ffn_out(x, o_p, o_s, wo, g3, wg, wu, wd, l, i, n_seq, t_pad, n_meta, seq):
    rows, d = x.shape
    ffn = wg.shape[-1]
    rows_p = n_seq * t_pad
    rows_s = rows - rows_p
    assert seq % ROW_TILE == 0 and rows_s % ROW_TILE == 0 and o_p.shape[0] == rows_p and o_s.shape[0] == rows_s
    per_seq = seq // ROW_TILE
    n_p = n_seq * per_seq

    def weight(shape):
        return pl.BlockSpec((None, None) + shape, lambda _: (l, i, 0, 0), pipeline_mode=pl.Buffered(1))

    def flat_start(s):
        prompt = (s // per_seq) * t_pad + n_meta + (s % per_seq) * ROW_TILE
        return jnp.where(s < n_p, prompt, rows_p + (s - n_p) * ROW_TILE)

    def kept_rows(limit):
        return pl.BlockSpec((pl.Element(ROW_TILE), pl.Element(d)),
                            lambda s: (pl.multiple_of(jnp.minimum(flat_start(s), limit - ROW_TILE), SUBLANES), 0))

    samp = pl.BlockSpec((ROW_TILE, d), lambda s: (jnp.maximum(s - n_p, 0), 0))
    prompt_out = pl.BlockSpec((1, ROW_TILE, d), lambda s: (jnp.minimum(s // per_seq, n_seq - 1),
                                                           jnp.where(s < n_p, s % per_seq, per_seq - 1), 0))
    return pl.pallas_call(
        functools.partial(_ffn_out_kernel, n_p),
        grid=(n_p + rows_s // ROW_TILE,),
        in_specs=[kept_rows(rows), kept_rows(rows_p), samp, _const_spec((d, d)), _const_spec((3, d)),
                  weight((d, ffn)), weight((d, ffn)), weight((ffn, d))],
        out_specs=[prompt_out, samp],
        out_shape=[jax.ShapeDtypeStruct((n_seq, seq, d), f32), jax.ShapeDtypeStruct((rows_s, d), f32)],
        compiler_params=_cparams(1),
        name="ffn_out",
    )(x, o_p, o_s, wo, g3, wg, wu, wd)


def _kv_kernel(n_prompt_tiles, x_ref, g_ref, wk_ref, wv_ref, kp_ref, vp_ref, ks_ref, vs_ref, kb_ref, vb_ref):
    s = pl.program_id(0)
    hk = _rms(x_ref[...], g_ref[...], NORM_EPS).astype(bf16)
    k = _dot(hk, wk_ref[...])
    v = _dot(hk, wv_ref[...])
    kb_ref[...] = k.astype(bf16)
    vb_ref[...] = v.astype(bf16)

    @pl.when(s < n_prompt_tiles)
    def _prompt():
        kp_ref[0] = k
        vp_ref[0] = v

    @pl.when(s >= n_prompt_tiles)
    def _sample():
        ks_ref[...] = k
        vs_ref[...] = v


def _kv(x, g, wk, wv, n_seq, t_pad, t_real):
    rows, d = x.shape
    n_seq_tiles = t_pad // SEQ_TILE
    n_prompt_tiles = n_seq * n_seq_tiles
    rows_s = rows - n_prompt_tiles * SEQ_TILE
    tile = pl.BlockSpec((SEQ_TILE, d), lambda s: (s, 0))
    prompt = pl.BlockSpec((1, SEQ_TILE, d),
                          lambda s: (jnp.minimum(s // n_seq_tiles, n_seq - 1),
                                     jnp.where(s < n_prompt_tiles, s % n_seq_tiles, n_seq_tiles - 1), 0))
    samp = pl.BlockSpec((SEQ_TILE, d), lambda s: (jnp.maximum(s - n_prompt_tiles, 0), 0))
    return pl.pallas_call(
        functools.partial(_kv_kernel, n_prompt_tiles),
        grid=(rows // SEQ_TILE,),
        in_specs=[tile, _const_spec((1, d)), _const_spec((d, d)), _const_spec((d, d))],
        out_specs=[prompt, prompt, samp, samp, tile, tile],
        out_shape=[jax.ShapeDtypeStruct((n_seq, t_real, d), f32)] * 2
        + [jax.ShapeDtypeStruct((rows_s, d), f32)] * 2
        + [jax.ShapeDtypeStruct((rows, d), bf16)] * 2,
        compiler_params=_cparams(1),
        name="kv_proj",
    )(x, g, wk, wv)


def _ffn_q_kernel(x_ref, g_ref, wg_ref, wu_ref, wd_ref, wq_ref, o_ref, q_ref):
    x = x_ref[...]
    xn = _rms(x, g_ref[0:1, :], NORM_EPS).astype(bf16)
    act = (jax.nn.silu(_dot(xn, wg_ref[...])) * _dot(xn, wu_ref[...])).astype(bf16)
    y = x + 0.5 * _rms(_dot(act, wd_ref[...]), g_ref[1:2, :], NORM_EPS)
    o_ref[...] = y
    h = _rms(y, g_ref[2:3, :], NORM_EPS).astype(bf16)
    q_ref[...] = _dot(h, wq_ref[...]) * (HEAD_DIM ** -0.5)


def _ffn_q(x, g3, wg, wu, wd, wq, l, i):
    rows, d = x.shape
    ffn = wg.shape[-1]

    def weight(shape):
        return pl.BlockSpec((None, None) + shape, lambda _: (l, i, 0, 0), pipeline_mode=pl.Buffered(1))

    row = pl.BlockSpec((ROW_TILE, d), lambda i: (i, 0))
    return pl.pallas_call(
        _ffn_q_kernel,
        grid=(rows // ROW_TILE,),
        in_specs=[row, _const_spec((3, d)), weight((d, ffn)), weight((d, ffn)), weight((ffn, d)),
                  _const_spec((d, d))],
        out_specs=[row, row],
        out_shape=[jax.ShapeDtypeStruct((rows, d), f32)] * 2,
        compiler_params=_cparams(1),
        name="ffn_q",
    )(x, g3, wg, wu, wd, wq)


def _log_sigmoid(x):
    return -(jnp.maximum(-x, 0.0) + jnp.log1p(jnp.exp(-jnp.abs(x))))


def _sigmoid(x):
    return 0.5 * jnp.tanh(0.5 * x) + 0.5


def _scan_groups(a, b):
    rows, c = a.shape
    a = a.reshape(rows // SUBLANES, SUBLANES, c)
    b = b.reshape(rows // SUBLANES, SUBLANES, c)
    t = lax.broadcasted_iota(jnp.int32, (1, SUBLANES, 1), 1)
    s = 1
    while s < SUBLANES:
        keep = t >= s
        a_s = jnp.where(keep, pltpu.roll(a, s, 1), 1.0)
        b_s = jnp.where(keep, pltpu.roll(b, s, 1), 0.0)
        b = a * b_s + b
        a = a * a_s
        s *= 2
    return a.reshape(rows, c), b.reshape(rows, c)


def _lru_kernel(n_seq_tiles, n_prompt_tiles, last_tile, last_row,
                x_ref, cb_ref, h0_ref, g_ref, wy_ref, wx_ref, cw_ref, vec_ref, wa_ref, wi_ref, wo_ref,
                xo_ref, hp_ref, cp_ref, hs_ref, us_ref, uext_ref, hcar_ref):
    s = pl.program_id(0)
    rows, c = x_ref.shape
    bw = c // LRU_BLOCKS
    conv_b, ba, bx, lam = (vec_ref[i:i + 1, :] for i in range(4))

    def project_in():
        hn = _rms(x_ref[...], g_ref[0:1, :], NORM_EPS).astype(bf16)
        return _dot(hn, wx_ref[...]), hn

    def project_out(h, hn):
        hy = (h * jax.nn.gelu(_dot(hn, wy_ref[...]))).astype(bf16)
        xo_ref[...] = x_ref[...] + _rms(_dot(hy, wo_ref[...]), g_ref[1:2, :], NORM_EPS)

    def gates_and_ab(uc):
        ucb = uc.astype(bf16)
        rp = jnp.concatenate([_dot(ucb[:, n * bw:(n + 1) * bw], wa_ref[n]) for n in range(LRU_BLOCKS)], axis=1)
        ip = jnp.concatenate([_dot(ucb[:, n * bw:(n + 1) * bw], wi_ref[n]) for n in range(LRU_BLOCKS)], axis=1)
        r = jax.nn.sigmoid(rp + ba)
        i = _sigmoid(ip + bx)
        log_a = LRU_C * r * _log_sigmoid(lam)
        a = jnp.exp(log_a)
        y = -jnp.tanh(log_a) * (a * a + 1.0)
        b = jnp.where(y > 0.0, y * lax.rsqrt(y), 0.0) * (i * uc)
        return a, b

    @pl.when(s < n_prompt_tiles)
    def _prompt():
        i = jnp.where(s >= n_seq_tiles, s - n_seq_tiles, s)

        @pl.when(i == 0)
        def _():
            uext_ref[0:SUBLANES, :] = jnp.zeros((SUBLANES, c), f32)
            hcar_ref[...] = jnp.zeros_like(hcar_ref)

        u, hn = project_in()
        uext_ref[SUBLANES:SUBLANES + rows, :] = u
        uc = conv_b + cw_ref[CONV_W - 1:CONV_W, :] * u
        for j in range(1, CONV_W):
            uc = uc + cw_ref[CONV_W - 1 - j:CONV_W - j, :] * uext_ref[SUBLANES - j:SUBLANES - j + rows, :]
        a, b = gates_and_ab(uc)
        a, b = _scan_groups(a, b)
        carry = hcar_ref[...]
        groups = []
        for g in range(rows // SUBLANES):
            hg = a[g * SUBLANES:(g + 1) * SUBLANES] * carry + b[g * SUBLANES:(g + 1) * SUBLANES]
            carry = hg[SUBLANES - 1:SUBLANES]
            groups.append(hg)
        h = jnp.concatenate(groups, axis=0)
        project_out(h, hn)
        uext_ref[0:SUBLANES, :] = u[rows - SUBLANES:rows]
        hcar_ref[...] = carry

        @pl.when(i == last_tile)
        def _():
            grp = last_row // SUBLANES * SUBLANES
            hp_ref[0] = h[last_row:last_row + 1]
            cp_ref[0] = u[grp:grp + SUBLANES]

    @pl.when(s >= n_prompt_tiles)
    def _sample():
        u, hn = project_in()
        t = lax.broadcasted_iota(jnp.int32, (rows, 1), 0) % SUBLANES
        cb = cb_ref[...]
        uc = conv_b + cw_ref[CONV_W - 1:CONV_W, :] * u
        for j in range(1, CONV_W):
            uj = jnp.where(t >= j, pltpu.roll(u, j, 0), pltpu.roll(cb, rows - SUBLANES + j, 0))
            uc = uc + cw_ref[CONV_W - 1 - j:CONV_W - j, :] * uj
        a, b = gates_and_ab(uc)
        a, b = _scan_groups(a, b)
        h = a * h0_ref[...] + b
        project_out(h, hn)
        hs_ref[...] = h
        us_ref[...] = u


def _lru_layer(x, cb_rows, h0_rows, g2, wy, wx, conv_w, vecs, wa, wi, wo, n_seq, t_pad, t_real):
    rows, c = x.shape
    n_seq_tiles = t_pad // SEQ_TILE
    n_prompt_tiles = n_seq * n_seq_tiles
    n_tiles = rows // SEQ_TILE
    rows_s = rows - n_prompt_tiles * SEQ_TILE
    bw = c // LRU_BLOCKS
    last_row = (t_real - 1) % SEQ_TILE
    assert last_row % SUBLANES >= CONV_W - 2
    tile = pl.BlockSpec((SEQ_TILE, c), lambda s: (s, 0))
    samp = pl.BlockSpec((SEQ_TILE, c), lambda s: (jnp.maximum(s - n_prompt_tiles, 0), 0))

    def per_seq(n):
        return pl.BlockSpec((1, n, c), lambda s: (jnp.minimum(s // n_seq_tiles, n_seq - 1), 0, 0))

    kern = functools.partial(_lru_kernel, n_seq_tiles, n_prompt_tiles, (t_real - 1) // SEQ_TILE, last_row)
    return pl.pallas_call(
        kern,
        grid=(n_tiles,),
        in_specs=[tile, samp, samp, _const_spec((2, c)), _const_spec((c, c)), _const_spec((c, c)),
                  _const_spec((CONV_W, c)), _const_spec((4, c)),
                  _const_spec((LRU_BLOCKS, bw, bw)), _const_spec((LRU_BLOCKS, bw, bw)), _const_spec((c, c))],
        out_specs=[tile, per_seq(1), per_seq(SUBLANES), samp, samp],
        out_shape=[jax.ShapeDtypeStruct((rows, c), f32),
                   jax.ShapeDtypeStruct((n_seq, 1, c), f32),
                   jax.ShapeDtypeStruct((n_seq, SUBLANES, c), f32),
                   jax.ShapeDtypeStruct((rows_s, c), f32),
                   jax.ShapeDtypeStruct((rows_s, c), f32)],
        scratch_shapes=[pltpu.VMEM((SUBLANES + SEQ_TILE, c), f32), pltpu.VMEM((1, c), f32)],
        compiler_params=_cparams(1),
        name="lru_layer",
    )(x, cb_rows, h0_rows, g2, wy, wx, conv_w, vecs, wa, wi, wo)


def _bucket_last_dist():
    n = np.arange(0, 4 * MAX_DIST, dtype=np.int32)
    max_exact = N_BUCKETS // 2
    nf = np.maximum(n, 1).astype(np.float32)
    large = max_exact + (np.log(nf / np.float32(max_exact)) / np.float32(math.log(MAX_DIST / max_exact))
                         * np.float32(N_BUCKETS - max_exact)).astype(np.int32)
    bucket = np.where(n < max_exact, n, np.minimum(large, N_BUCKETS - 1))
    assert np.all(np.diff(bucket) >= 0)
    last = [int(n[bucket == b].max()) if np.any(bucket == b) else None for b in range(N_BUCKETS)]
    assert last[N_BUCKETS - 1] == n[-1]
    return last


_BUCKET_LAST = _bucket_last_dist()
FAR_DIST = max(d for d in _BUCKET_LAST[:-1] if d is not None) + 1


def _bias_of_dist(dist, rb_ref, h):
    val = jnp.full(dist.shape, rb_ref[N_BUCKETS - 1, h], f32)
    for b in range(N_BUCKETS - 2, -1, -1):
        if _BUCKET_LAST[b] is not None:
            val = jnp.where(dist <= _BUCKET_LAST[b], rb_ref[b, h], val)
    return val


def _bias_kernel(past_len, rb_ref, g_ref, bs_ref):
    h = pl.program_id(0)
    _, nk, tq = g_ref.shape
    ki = lax.broadcasted_iota(jnp.int32, (nk, tq), 0)
    qi = lax.broadcasted_iota(jnp.int32, (nk, tq), 1)
    g_ref[0] = _bias_of_dist(qi - ki + (nk - tq), rb_ref, h)
    _, _, nq, wk = bs_ref.shape
    q = lax.broadcasted_iota(jnp.int32, (nq, wk), 0)
    col = lax.broadcasted_iota(jnp.int32, (nq, wk), 1)
    bias = _bias_of_dist(past_len + q - col, rb_ref, h)
    bs_ref[0, 0] = bias
    bs_ref[1, 0] = bias


def _bias_tables(rel_bias, past_len, n_new):
    tq = ATT_CHAINS * ATT_CHAIN
    g_shape = (N_HEADS, ATT_TILE + tq, tq)
    bs_shape = (2, N_HEADS, n_new, past_len + PAGE)
    return pl.pallas_call(
        functools.partial(_bias_kernel, past_len),
        grid=(N_HEADS,),
        in_specs=[pl.BlockSpec(memory_space=pltpu.SMEM)],
        out_specs=[pl.BlockSpec((1,) + g_shape[1:], lambda h: (h, 0, 0)),
                   pl.BlockSpec((2, 1) + bs_shape[2:], lambda h: (0, h, 0, 0))],
        out_shape=[jax.ShapeDtypeStruct(g_shape, f32), jax.ShapeDtypeStruct(bs_shape, f32)],
        compiler_params=_cparams(1),
        name="rel_bias_tables",
    )(rel_bias)


def _lambda(lamv_ref, lam_init):
    l1 = jnp.sum(lamv_ref[0:1, :] * lamv_ref[1:2, :], axis=-1, keepdims=True)
    l2 = jnp.sum(lamv_ref[2:3, :] * lamv_ref[3:4, :], axis=-1, keepdims=True)
    return jnp.exp(l1) - jnp.exp(l2) + lam_init


def _loop(n, body, init):
    if isinstance(n, int) and n == 0:
        return init
    return lax.fori_loop(0, n, body, init)


def _rows(start, n):
    if isinstance(start, int):
        return pl.ds(start, n)
    return pl.ds(pl.multiple_of(start, SEQ_TILE), n)


def _attn_prompt_kernel(lam_init, rb_ref, q_ref, k_ref, v_ref, g_ref, lamv_ref, sgc_ref, o_ref, vt_ref, s_ref):
    h = pl.program_id(1)
    t_pad, hw = q_ref.shape
    tk = ATT_TILE
    tq = ATT_CHAINS * ATT_CHAIN
    n_kblk, k_tail = divmod(t_pad, tk)
    far_bias = rb_ref[N_BUCKETS - 1, h]
    lam = _lambda(lamv_ref, lam_init)
    sgc = sgc_ref[...]

    ones_rows = (lax.broadcasted_iota(jnp.int32, (ONES_ROWS, tk), 0) == 0).astype(bf16)

    def fill_vt(j, n):
        vt_ref[j, 0:hw, 0:n] = v_ref[_rows(j * tk, n), :].astype(f32).T.astype(bf16)
        vt_ref[j, hw:hw + ONES_ROWS, :] = ones_rows

    def fill_body(j, carry):
        fill_vt(j, tk)
        return carry

    lax.fori_loop(0, n_kblk, fill_body, 0)
    if k_tail:
        fill_vt(n_kblk, k_tail)

    def chain_q(q0, c, tc):
        row = lax.broadcasted_iota(jnp.int32, (hw, tc), 0)
        qt = q_ref[_rows(q0 + c * tc, tc), :].T
        return jnp.concatenate([jnp.where(row < HEAD_DIM, qt, 0.0), jnp.where(row >= HEAD_DIM, qt, 0.0)],
                               axis=1).astype(bf16)

    def update(state, s, off, vt):
        m, acc = state
        smax = jnp.max(s, axis=0, keepdims=True)
        m_new = jnp.maximum(m, smax if off is None else smax + off)
        alpha = jnp.exp(m - m_new)
        p = jnp.exp(s - (m_new if off is None else m_new - off))
        acc = alpha * acc + _dot(vt, p.astype(bf16))
        return m_new, acc

    def init_states(n_chain, tc):
        return tuple((jnp.full((1, 2 * tc), NEG_INF, f32), jnp.zeros((hw + ONES_ROWS, 2 * tc), f32))
                     for _ in range(n_chain))

    def far_part(q0, n_chain, tc, n_prev):
        qq = [chain_q(q0, c, tc) for c in range(n_chain)]

        def logits(j, slot):
            kb = k_ref[_rows(j * tk, tk), :]
            for c in range(n_chain):
                s_ref[slot, c, :, 0:2 * tc] = _dot(kb, qq[c])

        def far_step(j, slot, states):
            logits(j + 1, 1 - slot)
            vt = vt_ref[j]
            return tuple(update(states[c], s_ref[slot, c, :, 0:2 * tc], far_bias, vt) for c in range(n_chain))

        def far_pair(j, st):
            return far_step(j + 1, 1, far_step(j, 0, st))

        n_far = n_prev - 1
        n_quad = n_far >> 2
        logits(0, 0)
        states = _loop(n_quad, lambda i, st: far_pair(4 * i + 2, far_pair(4 * i, st)), init_states(n_chain, tc))
        states = _loop((n_far >> 1) & 1, lambda i, st: far_pair(4 * n_quad, st), states)
        return _loop(n_far & 1, lambda i, st: far_step(n_far - 1, 0, st), states)

    def close_part(q0, n_chain, tc, n_prev, has_near, states):
        qq = [chain_q(q0, c, tc) for c in range(n_chain)]
        for c in range(n_chain):
            nk = (c + 1) * tc
            cols = slice(c * tc, (c + 1) * tc)
            g = g_ref[0, tk:tk + nk, cols]
            ki = lax.broadcasted_iota(jnp.int32, (nk, 2 * tc), 0)
            qi = c * tc + (lax.broadcasted_iota(jnp.int32, (nk, 2 * tc), 1) & (tc - 1))
            s = _dot(k_ref[_rows(q0, nk), :], qq[c]) + jnp.concatenate([g, g], axis=1)
            s = jnp.where(ki <= qi, s, NEG_INF)
            vts = [vt_ref[n_prev + b] for b in range(nk // tk)]
            if nk % tk:
                vts.append(vt_ref[n_prev + nk // tk][:, 0:nk % tk])
            if has_near:
                g = g_ref[0, 0:tk, cols]
                near = s_ref[(n_prev - 1) & 1, c, :, 0:2 * tc] + jnp.concatenate([g, g], axis=1)
                s = jnp.concatenate([near, s], axis=0)
                vts = [vt_ref[n_prev - 1]] + vts
            vt = vts[0] if len(vts) == 1 else jnp.concatenate(vts, axis=1)
            _, acc = update(states[c], s, None, vt)
            on = acc[0:hw] * (1.0 / acc[hw:hw + 1])
            ot = on[:, 0:tc] - lam * on[:, tc:2 * tc]
            ms = jnp.mean(ot * ot, axis=0, keepdims=True)
            ot = ot * lax.rsqrt(ms + SUBLN_EPS) * sgc * (1.0 - lam_init)
            o_ref[_rows(q0 + c * tc, tc), :] = ot.T.astype(o_ref.dtype)

    n_tiles, q_tail = divmod(t_pad, tq)
    nc, tc, per_tile = ATT_CHAINS, ATT_CHAIN, tq // tk
    assert n_tiles >= 2 and q_tail % SEQ_TILE == 0 and q_tail <= ATT_CHAIN
    close_part(0, nc, tc, 0, False, init_states(nc, tc))
    states = far_part(tq, nc, tc, per_tile)

    def tile_body(p, states):
        close_part((p - 1) * tq, nc, tc, (p - 1) * per_tile, True, states)
        return far_part(p * tq, nc, tc, p * per_tile)

    states = lax.fori_loop(2, n_tiles, tile_body, states)
    close_part((n_tiles - 1) * tq, nc, tc, (n_tiles - 1) * per_tile, True, states)
    if q_tail:
        states = far_part(n_tiles * tq, 1, q_tail, n_tiles * per_tile)
        close_part(n_tiles * tq, 1, q_tail, n_tiles * per_tile, True, states)


def _attn_prompt(q, kb, vb, g_tiles, rel_bias, lamv, subln_g, n_seq, t_pad, lam_init):
    rows, d = q.shape
    hw = d // N_HEADS
    tq = ATT_CHAINS * ATT_CHAIN
    assert ATT_TILE >= FAR_DIST and tq % ATT_TILE == 0 and t_pad % SEQ_TILE == 0
    blk = lambda b, h: (b, h)
    sgc = subln_g.reshape(hw, 1)
    return pl.pallas_call(
        functools.partial(_attn_prompt_kernel, lam_init),
        grid=(n_seq, N_HEADS),
        in_specs=[pl.BlockSpec(memory_space=pltpu.SMEM),
                  pl.BlockSpec((t_pad, hw), blk), pl.BlockSpec((t_pad, hw), blk),
                  pl.BlockSpec((t_pad, hw), blk),
                  pl.BlockSpec((1,) + g_tiles.shape[1:], lambda b, h: (h, 0, 0), pipeline_mode=pl.Buffered(1)),
                  _const_spec(lamv.shape), _const_spec(sgc.shape)],
        out_specs=pl.BlockSpec((t_pad, hw), blk),
        out_shape=jax.ShapeDtypeStruct((n_seq * t_pad, d), bf16),
        scratch_shapes=[pltpu.VMEM((pl.cdiv(t_pad, ATT_TILE), hw + ONES_ROWS, ATT_TILE), bf16),
                        pltpu.VMEM((2, ATT_CHAINS, ATT_TILE, 2 * ATT_CHAIN), f32)],
        compiler_params=_cparams(2),
        name="attn_prompt",
    )(rel_bias, q, kb, vb, g_tiles, lamv, sgc)


def _attn_sample_kernel(lam_init, n_pages, pt_ref, q_ref, kn_ref, vn_ref, bias_ref, lamv_ref, sg_ref,
                        *refs):
    k_refs = refs[:n_pages]
    v_refs = refs[n_pages:2 * n_pages]
    o_ref = refs[2 * n_pages]
    n_new, d = q_ref.shape
    n_grp = 2 * N_HEADS
    n_rows = n_grp * n_new
    hw = d // N_HEADS
    lam = _lambda(lamv_ref, lam_init)

    qt = jnp.concatenate([q_ref[...]] * n_grp, axis=0)
    r = lax.broadcasted_iota(jnp.int32, (n_rows, d), 0)
    col = lax.broadcasted_iota(jnp.int32, (n_rows, d), 1)
    row_grp = ((r // n_new) % N_HEADS) * 2 + r // (N_HEADS * n_new)
    wq = jnp.where(col // HEAD_DIM == row_grp, qt, 0.0).astype(bf16)

    pad = jnp.zeros((PAGE - n_new, d), f32)
    k_new = jnp.concatenate([kn_ref[...], pad], axis=0).astype(bf16)
    v_new = jnp.concatenate([vn_ref[...], pad], axis=0).astype(bf16)

    s = jnp.concatenate([_dot(wq, k_refs[j][0].astype(bf16)) for j in range(n_pages)]
                        + [_dot_nt(wq, k_new)], axis=1)
    s = s + bias_ref[...]
    past = n_pages * PAGE
    rr = lax.broadcasted_iota(jnp.int32, s.shape, 0)
    cc = lax.broadcasted_iota(jnp.int32, s.shape, 1)
    s = jnp.where(cc - past <= rr % n_new, s, NEG_INF)
    m = jnp.max(s, axis=-1, keepdims=True)
    p = jnp.exp(s - m)
    p = p / jnp.sum(p, axis=-1, keepdims=True)
    half = N_HEADS * n_new
    w = (p[0:half] - lam * p[half:2 * half]).astype(bf16)
    sg = sg_ref[...]
    heads = []
    for hh in range(N_HEADS):
        vh = jnp.concatenate([v_refs[j][0, pl.ds(hh, PAGE, stride=N_HEADS), :].astype(bf16)
                              for j in range(n_pages)] + [v_new[:, hh * hw:(hh + 1) * hw]], axis=0)
        out = _dot(w[hh * n_new:(hh + 1) * n_new, :], vh)
        heads.append(_rms(out, sg, SUBLN_EPS))
    o_ref[...] = jnp.concatenate(heads, axis=1) * (1.0 - lam_init)


def _attn_sample(q, k_new, v_new, cache_k, cache_v, page_table, bias_s, lamv, subln_g, row0, n_new, lam_init):
    n_batch, n_pages = page_table.shape
    d = q.shape[1]
    blk0 = row0 // n_new
    tok = pl.BlockSpec((n_new, d), lambda b, pt: (blk0 + b, 0))
    new = pl.BlockSpec((n_new, d), lambda b, pt: (b, 0))

    def page_spec(j):
        return pl.BlockSpec((1,) + cache_k.shape[1:], lambda b, pt: (pt[b, j], 0, 0))

    grid_spec = pltpu.PrefetchScalarGridSpec(
        num_scalar_prefetch=1,
        grid=(n_batch,),
        in_specs=[tok, new, new,
                  pl.BlockSpec(bias_s.shape, lambda b, pt: (0, 0), pipeline_mode=pl.Buffered(1)),
                  pl.BlockSpec(lamv.shape, lambda b, pt: (0, 0), pipeline_mode=pl.Buffered(1)),
                  pl.BlockSpec(subln_g.shape, lambda b, pt: (0, 0), pipeline_mode=pl.Buffered(1))]
        + [page_spec(j) for j in range(n_pages)] * 2,
        out_specs=new,
    )
    return pl.pallas_call(
        functools.partial(_attn_sample_kernel, lam_init, n_pages),
        grid_spec=grid_spec,
        out_shape=jax.ShapeDtypeStruct((n_batch * n_new, d), f32),
        compiler_params=_cparams(1),
        name="attn_sample",
    )(page_table, q, k_new, v_new, bias_s, lamv, subln_g,
      *([cache_k] * n_pages), *([cache_v] * n_pages))


def kernel(x_prompt, x_sample, cache_k, cache_v, page_table, state_conv, state_h, meta_tokens, norm_g, kv_norm_g, ffn_w_gate, ffn_w_up, ffn_w_down, lru_w_y, lru_w_x, lru_conv_w, lru_conv_b, lru_wa, lru_ba, lru_wx, lru_bx, lru_lambda, lru_w_out, attn_w_q, attn_w_k, attn_w_v, attn_w_o, attn_lambda_q1, attn_lambda_k1, attn_lambda_q2, attn_lambda_k2, attn_subln_g, rel_bias):
    n_seq, seq, d = x_prompt.shape
    n_batch, n_new, _ = x_sample.shape
    n_pool, page, _, _, _ = cache_k.shape
    n_pages = page_table.shape[1]
    depth = norm_g.shape[0]
    assert page == PAGE and n_new == SUBLANES and depth == 2 and lru_w_y.shape[0] == 1
    t_real = seq + N_META
    t_pad = -(-t_real // SEQ_TILE) * SEQ_TILE
    rows_p = n_seq * t_pad
    rows = rows_p + n_batch * n_new
    assert rows % ROW_TILE == 0 and rows_p % SEQ_TILE == 0 and (rows - rows_p) % SEQ_TILE == 0
    past_len = n_pages * PAGE

    meta = meta_tokens.astype(x_prompt.dtype)
    zpad = jnp.zeros((t_pad - t_real, d), x_prompt.dtype)
    parts = []
    for b in range(n_seq):
        parts += [meta, x_prompt[b], zpad]
    x = jnp.concatenate(parts + [x_sample.reshape(n_batch * n_new, d)], axis=0)

    wg_all, wu_all, wd_all = (w.astype(bf16) for w in (ffn_w_gate, ffn_w_up, ffn_w_down))

    def ffn(x, l, i):
        return _ffn(x, norm_g[l, 4 * i:4 * i + 2], wg_all, wu_all, wd_all, l, i)

    x = ffn(x, 0, 0)
    cb_rows = jnp.pad(state_conv[0], ((0, 0), (SUBLANES - (CONV_W - 1), 0), (0, 0))).reshape(n_batch * n_new, d)
    h0_rows = jnp.repeat(state_h[0], n_new, axis=0)
    vecs = jnp.stack([lru_conv_b[0], lru_ba[0], lru_bx[0], lru_lambda[0]])
    x, h_p, u_p, hs, u_s = _lru_layer(
        x, cb_rows, h0_rows, norm_g[0, 2:4], lru_w_y[0].astype(bf16), lru_w_x[0].astype(bf16), lru_conv_w[0],
        vecs, lru_wa[0].astype(bf16), lru_wx[0].astype(bf16), lru_w_out[0].astype(bf16), n_seq, t_pad, t_real)
    x = ffn(x, 0, 1)
    k_p, v_p, k_s, v_s, kb, vb = _kv(x, kv_norm_g[None], attn_w_k.astype(bf16), attn_w_v.astype(bf16),
                                     n_seq, t_pad, t_real)

    lam_init = 0.8 - 0.6 * math.exp(-0.3 * 1)
    x, q = _ffn_q(x, norm_g[1, 0:3], wg_all, wu_all, wd_all, attn_w_q[0].astype(bf16), 1, 0)
    g_tiles, bias_s = _bias_tables(rel_bias, past_len, n_new)
    bias_s = bias_s.reshape(2 * N_HEADS * n_new, past_len + PAGE)
    lamv = jnp.stack([attn_lambda_q1[0], attn_lambda_k1[0], attn_lambda_q2[0], attn_lambda_k2[0]])
    sg = attn_subln_g[0][None]
    o_p = _attn_prompt(q, kb, vb, g_tiles, rel_bias, lamv, sg, n_seq, t_pad, lam_init)
    cache_kt = jnp.transpose(cache_k, (0, 2, 3, 4, 1)).reshape(n_pool, d, PAGE)
    cache_vr = cache_v.reshape(n_pool, PAGE * N_HEADS, d // N_HEADS)
    o_s = _attn_sample(q, k_s, v_s, cache_kt, cache_vr, page_table, bias_s, lamv, sg, rows_p, n_new, lam_init)
    y_prompt, y_s = _ffn_out(x, o_p, o_s, attn_w_o[0].astype(bf16), norm_g[1, 3:6], wg_all, wu_all, wd_all, 1, 1,
                             n_seq, t_pad, N_META, seq)


    y_sample = y_s.reshape(n_batch, n_new, d)
    k_p = k_p.reshape(n_seq, t_real, N_HEADS, 2, HEAD_DIM)
    v_p = v_p.reshape(n_seq, t_real, N_HEADS, 2 * HEAD_DIM)
    last = (t_real - 1) % SUBLANES
    conv_p = u_p[:, last - (CONV_W - 2):last + 1][None]
    h_p = h_p.reshape(1, n_seq, d)
    k_s = k_s.reshape(n_batch, n_new, N_HEADS, 2, HEAD_DIM)
    v_s = v_s.reshape(n_batch, n_new, N_HEADS, 2 * HEAD_DIM)
    conv_s = u_s.reshape(n_batch, n_new, d)[:, n_new - (CONV_W - 1):][None]
    h_s = hs.reshape(n_batch, n_new, d)[:, n_new - 1][None]
    return (y_prompt, y_sample, k_p, v_p, conv_p, h_p, k_s, v_s, conv_s, h_s)
```
